```python
import jax, jax.numpy as jnp
from jax import lax
import numpy as np

D_MODEL = 1024
BATCH = 16
SEQ = 2048
DEPTH = 2

GRID_W = 64
CTX_LEN = 256
EPS = 1e-6

A_GROUPS = 4
A_GROUP_DIM = 128
D_A = A_GROUPS * A_GROUP_DIM
D_B = 512
CONV_WIDTH = 31
EVEN_IN = D_A + 2 * D_B
EVEN_OUT = D_A + D_B

HEAD_DIM = 64
N_Q_HEADS = 12
N_KV_HEADS = 4
Q_PER_KV = N_Q_HEADS // N_KV_HEADS
D_Q = N_Q_HEADS * HEAD_DIM
D_KV = N_KV_HEADS * HEAD_DIM
POOL_WINDOWS = (2, 4, 8, 16)
POOL_GROUPS = len(POOL_WINDOWS)
POOL_GROUP_DIM = 64
D_POOL = POOL_GROUPS * POOL_GROUP_DIM
ODD_IN = D_Q + 2 * D_KV + D_POOL
ODD_OUT = D_Q + D_POOL
Q_BLOCK = 128
ROPE_THETA = 10000.0
ROPE_PAIRS = HEAD_DIM // 4

D_FF = 2816
FFN_CONV_WIDTH = 3

kernel_name = "hybrid_fourier_conformer_gqa_pool_dit"


def rmsnorm(x, g):
    xf = x.astype(jnp.float32)
    y = xf * lax.rsqrt(jnp.mean(xf * xf, axis=-1, keepdims=True) + EPS)
    return (y * g.astype(jnp.float32)).astype(x.dtype)


def layernorm(x, g, b):
    xf = x.astype(jnp.float32)
    mu = jnp.mean(xf, axis=-1, keepdims=True)
    var = jnp.mean(jnp.square(xf - mu), axis=-1, keepdims=True)
    y = (xf - mu) * lax.rsqrt(var + EPS)
    return (y * g.astype(jnp.float32) + b.astype(jnp.float32)).astype(x.dtype)


def modulate(h, shift, scale):
    return h * (1 + scale) + shift


def dwconv(x, w):
    C = x.shape[-1]
    return lax.conv_general_dilated(
        x, w[:, None, :].astype(x.dtype), window_strides=(1,), padding='SAME',
        dimension_numbers=('NWC', 'WIO', 'NWC'), feature_group_count=C)


def rope_axis(x, cos, sin):
    x1, x2 = jnp.split(x, 2, axis=-1)
    return jnp.concatenate([x1 * cos - x2 * sin, x2 * cos + x1 * sin], axis=-1)


def apply_axial_rope(x, rope):
    cr, sr, cc, sc = [t.astype(x.dtype) for t in rope]
    half = HEAD_DIM // 2
    return jnp.concatenate([rope_axis(x[..., :half], cr, sr),
                            rope_axis(x[..., half:], cc, sc)], axis=-1)


def even_mixer(h, w_in, conv_w, ln_g, ln_b, w_out):
    Bn, L, _ = h.shape
    p = h @ w_in
    a = p[..., :D_A].reshape(Bn, L, A_GROUPS, A_GROUP_DIM)
    fa = jnp.fft.fft2(a.astype(jnp.float32), axes=(1, 3), norm='ortho').real
    fa = fa.astype(h.dtype).reshape(Bn, L, D_A)
    u = p[..., D_A:D_A + D_B]
    g = p[..., D_A + D_B:]
    b = u * jax.nn.sigmoid(g)
    b = jax.nn.silu(layernorm(dwconv(b, conv_w), ln_g, ln_b))
    return jnp.concatenate([fa, b], axis=-1) @ w_out


def gqa_softmax(q, k, v):
    Bn, Lq = q.shape[0], q.shape[1]
    qg = q.reshape(Bn, Lq, N_KV_HEADS, Q_PER_KV, HEAD_DIM).astype(jnp.float32)
    s = jnp.einsum('bqhgd,bkhd->bhgqk', qg, k.astype(jnp.float32)) * (HEAD_DIM ** -0.5)
    pr = jax.nn.softmax(s, axis=-1)
    o = jnp.einsum('bhgqk,bkhd->bqhgd', pr.astype(v.dtype), v)
    return o.reshape(Bn, Lq, D_Q)


def blocked_attention(q, k_all, v_all):
    Bn, S = q.shape[0], q.shape[1]
    nb = S // Q_BLOCK
    qb = q.reshape(Bn, nb, Q_BLOCK, N_Q_HEADS, HEAD_DIM).swapaxes(0, 1)
    o = lax.map(lambda qblk: gqa_softmax(qblk, k_all, v_all), qb)
    return o.swapaxes(0, 1).reshape(Bn, S, D_Q)


def multiscale_pool(u, w_pool, scale):
    Bn, L, _ = u.shape
    uf = u.astype(jnp.float32)
    cs = jnp.concatenate([jnp.zeros((Bn, 1, D_POOL), jnp.float32), jnp.cumsum(uf, axis=1)], axis=1)
    t = jnp.arange(L)
    outs = []
    for gi, w in enumerate(POOL_WINDOWS):
        sl = slice(gi * POOL_GROUP_DIM, (gi + 1) * POOL_GROUP_DIM)
        lo = jnp.maximum(t - w // 2, 0)
        hi = jnp.minimum(t + w // 2 - 1, L - 1) + 1
        cnt = (hi - lo).astype(jnp.float32)[:, None]
        csg = cs[..., sl]
        mean = (jnp.take(csg, hi, axis=1) - jnp.take(csg, lo, axis=1)) / cnt
        outs.append(mean - uf[..., sl])
    pooled = jnp.stack(outs, axis=2)
    mixed = jnp.einsum('blgc,gcd->blgd', pooled, w_pool.astype(jnp.float32)).reshape(Bn, L, D_POOL)
    return (mixed * scale.astype(jnp.float32)).astype(u.dtype)


def odd_mixer(h, hc, w_in, q_g, k_g, pool_w, pool_scale, w_out, rope, need_ctx):
    Bn, L, _ = h.shape
    p = h @ w_in
    q = p[..., :D_Q].reshape(Bn, L, N_Q_HEADS, HEAD_DIM)
    k = p[..., D_Q:D_Q + D_KV].reshape(Bn, L, N_KV_HEADS, HEAD_DIM)
    v = p[..., D_Q + D_KV:D_Q + 2 * D_KV].reshape(Bn, L, N_KV_HEADS, HEAD_DIM)
    u = p[..., D_Q + 2 * D_KV:]
    q = apply_axial_rope(rmsnorm(q, q_g), rope)
    k = apply_axial_rope(rmsnorm(k, k_g), rope)
    Lc = hc.shape[1]
    if need_ctx:
        pc = hc @ w_in
        kv_c = pc[..., D_Q:D_Q + 2 * D_KV]
    else:
        kv_c = hc @ w_in[:, D_Q:D_Q + 2 * D_KV]
    kc = rmsnorm(kv_c[..., :D_KV].reshape(Bn, Lc, N_KV_HEADS, HEAD_DIM), k_g)
    vc = kv_c[..., D_KV:].reshape(Bn, Lc, N_KV_HEADS, HEAD_DIM)
    k_all = jnp.concatenate([kc, k], axis=1)
    v_all = jnp.concatenate([vc, v], axis=1)
    attn = blocked_attention(q, k_all, v_all)
    y = jnp.concatenate([attn, multiscale_pool(u, pool_w, pool_scale)], axis=-1) @ w_out
    yc = None
    if need_ctx:
        qc = rmsnorm(pc[..., :D_Q].reshape(Bn, Lc, N_Q_HEADS, HEAD_DIM), q_g)
        attn_c = gqa_softmax(qc, kc, vc)
        pool_c = multiscale_pool(pc[..., D_Q + 2 * D_KV:], pool_w, pool_scale)
        yc = jnp.concatenate([attn_c, pool_c], axis=-1) @ w_out
    return y, yc


def conv_ffn(h, w_up, w_conv, w_down):
    u = dwconv(h @ w_up, w_conv)
    g, val = u[..., :D_FF], u[..., D_FF:]
    return (jax.nn.silu(g) * val) @ w_down


def setup_inputs(seed: int = 0) -> dict:
    key = jax.random.key(seed)
    ks = jax.random.split(key, 24)
    n_even = (DEPTH + 1) // 2
    n_odd = DEPTH // 2
    f32 = jnp.float32

    def nrm(k, shape, scale):
        return jax.random.normal(k, shape, f32) * scale

    def gain(k, shape):
        return 1.0 + 0.1 * jax.random.normal(k, shape, f32)

    return {
        'x': nrm(ks[0], (BATCH, SEQ, D_MODEL), 1.0),
        'c': nrm(ks[1], (BATCH, D_MODEL), 1.0),
        'ctx': nrm(ks[2], (BATCH, CTX_LEN, D_MODEL), 1.0),
        'c_ctx': nrm(ks[3], (D_MODEL,), 1.0),
        'w_ada': nrm(ks[4], (DEPTH, D_MODEL, 6 * D_MODEL), D_MODEL ** -0.5),
        'b_ada': nrm(ks[5], (DEPTH, 6 * D_MODEL), 0.02),
        'norm1_g': gain(ks[6], (DEPTH, D_MODEL)),
        'norm2_g': gain(ks[7], (DEPTH, D_MODEL)),
        'ev_w_in': nrm(ks[8], (n_even, D_MODEL, EVEN_IN), D_MODEL ** -0.5),
        'ev_conv_w': nrm(ks[9], (n_even, CONV_WIDTH, D_B), CONV_WIDTH ** -0.5),
        'ev_ln_g': gain(ks[10], (n_even, D_B)),
        'ev_ln_b': nrm(ks[11], (n_even, D_B), 0.02),
        'ev_w_out': nrm(ks[12], (n_even, EVEN_OUT, D_MODEL), EVEN_OUT ** -0.5),
        'od_w_in': nrm(ks[13], (n_odd, D_MODEL, ODD_IN), D_MODEL ** -0.5),
        'od_q_g': gain(ks[14], (n_odd, HEAD_DIM)),
        'od_k_g': gain(ks[15], (n_odd, HEAD_DIM)),
        'od_pool_w': nrm(ks[16], (n_odd, POOL_GROUPS, POOL_GROUP_DIM, POOL_GROUP_DIM), POOL_GROUP_DIM ** -0.5),
        'od_pool_scale': gain(ks[17], (n_odd, D_POOL)),
        'od_w_out': nrm(ks[18], (n_odd, ODD_OUT, D_MODEL), ODD_OUT ** -0.5),
        'ffn_w_up': nrm(ks[19], (DEPTH, D_MODEL, 2 * D_FF), D_MODEL ** -0.5),
        'ffn_conv_w': nrm(ks[20], (DEPTH, FFN_CONV_WIDTH, 2 * D_FF), FFN_CONV_WIDTH ** -0.5),
        'ffn_w_down': nrm(ks[21], (DEPTH, D_FF, D_MODEL), D_FF ** -0.5),
    }


def reference(x, c, ctx, c_ctx, w_ada, b_ada, norm1_g, norm2_g,
              ev_w_in, ev_conv_w, ev_ln_g, ev_ln_b, ev_w_out,
              od_w_in, od_q_g, od_k_g, od_pool_w, od_pool_scale, od_w_out,
              ffn_w_up, ffn_conv_w, ffn_w_down):
    S = x.shape[1]
    rows = S // GRID_W
    row_ids = jnp.repeat(jnp.arange(rows), GRID_W).astype(jnp.float32)
    col_ids = jnp.tile(jnp.arange(GRID_W), rows).astype(jnp.float32)
    freqs = ROPE_THETA ** (-jnp.arange(ROPE_PAIRS, dtype=jnp.float32) / ROPE_PAIRS)
    ang_r = (row_ids[:, None] * freqs)[:, None, :]
    ang_c = (col_ids[:, None] * freqs)[:, None, :]
    rope = (jnp.cos(ang_r), jnp.sin(ang_r), jnp.cos(ang_c), jnp.sin(ang_c))

    for i in range(DEPTH):
        last = i == DEPTH - 1
        j = i // 2
        mod = (jax.nn.silu(c) @ w_ada[i] + b_ada[i])[:, None, :]
        sh1, sc1, g1, sh2, sc2, g2 = jnp.split(mod, 6, axis=-1)
        modc = jax.nn.silu(c_ctx) @ w_ada[i] + b_ada[i]
        shc1, scc1, gc1, shc2, scc2, gc2 = jnp.split(modc, 6, axis=-1)

        h = modulate(rmsnorm(x, norm1_g[i]), sh1, sc1)
        hc = modulate(rmsnorm(ctx, norm1_g[i]), shc1, scc1)
        if i % 2 == 0:
            y = even_mixer(h, ev_w_in[j], ev_conv_w[j], ev_ln_g[j], ev_ln_b[j], ev_w_out[j])
            yc = None if last else even_mixer(hc, ev_w_in[j], ev_conv_w[j], ev_ln_g[j], ev_ln_b[j], ev_w_out[j])
        else:
            y, yc = odd_mixer(h, hc, od_w_in[j], od_q_g[j], od_k_g[j], od_pool_w[j],
                              od_pool_scale[j], od_w_out[j], rope, not last)
        x = x + g1 * y
        x = x + g2 * conv_ffn(modulate(rmsnorm(x, norm2_g[i]), sh2, sc2),
                              ffn_w_up[i], ffn_conv_w[i], ffn_w_down[i])
        if not last:
            ctx = ctx + gc1 * yc
            ctx = ctx + gc2 * conv_ffn(modulate(rmsnorm(ctx, norm2_g[i]), shc2, scc2),
                                       ffn_w_up[i], ffn_conv_w[i], ffn_w_down[i])
    return x
```

```python
import functools

import jax
import jax.numpy as jnp
from jax import lax
from jax.experimental import pallas as pl
from jax.experimental.pallas import tpu as pltpu

F32 = jnp.float32
BF16 = jnp.bfloat16

D_MODEL = 1024
GRID_W = 64
EPS = 1e-6

A_GROUPS = 4
A_GROUP_DIM = 128
D_A = A_GROUPS * A_GROUP_DIM
D_B = 512
CONV_WIDTH = 31
CONV_PAD = 16

HEAD_DIM = 64
N_Q_HEADS = 12
N_KV_HEADS = 4
D_Q = N_Q_HEADS * HEAD_DIM
D_KV = N_KV_HEADS * HEAD_DIM
POOL_WINDOWS = (2, 4, 8, 16)
POOL_GROUP_DIM = 64
D_POOL = len(POOL_WINDOWS) * POOL_GROUP_DIM
POOL_PAD = 16
ROPE_THETA = 10000.0
ROPE_PAIRS = HEAD_DIM // 4

D_FF = 2816
FFN_CHUNK = 256
FFN_HALO = 16

LANES = 128
VMEM_LIMIT = 56 * 1024 * 1024


def _cparams(*sem):
    return pltpu.CompilerParams(dimension_semantics=sem, vmem_limit_bytes=VMEM_LIMIT)


def _sigmoid(x):
    return 1.0 / (1.0 + jnp.exp(-x))


def _silu(x):
    return x * _sigmoid(x)


def _norm_mod(x, g, shift, scale):
    ms = jnp.mean(x * x, axis=-1, keepdims=True)
    y = x * lax.rsqrt(ms + EPS) * g
    return y * (1.0 + scale) + shift


def _dot(a, b):
    return jnp.dot(a, b, preferred_element_type=F32)


def _ada_kernel(cc_ref, w_ref, b_ref, o_ref):
    s = _silu(cc_ref[...])
    o_ref[0] = _dot(s.astype(BF16), w_ref[0].astype(BF16)) + b_ref[0]


def _ada(cc, w_ada, b_ada):
    depth = w_ada.shape[0]
    rows = cc.shape[0]
    nblk = w_ada.shape[2] // D_MODEL
    return pl.pallas_call(
        _ada_kernel,
        grid=(depth, nblk),
        in_specs=[
            pl.BlockSpec((rows, D_MODEL), lambda i, n: (0, 0)),
            pl.BlockSpec((1, D_MODEL, D_MODEL), lambda i, n: (i, 0, n)),
            pl.BlockSpec((1, 1, D_MODEL), lambda i, n: (i, 0, n)),
        ],
        out_specs=pl.BlockSpec((1, rows, D_MODEL), lambda i, n: (i, 0, n)),
        out_shape=jax.ShapeDtypeStruct((depth, rows, w_ada.shape[2]), F32),
        compiler_params=_cparams("parallel", "parallel"),
        name="ada",
    )(cc, w_ada, b_ada.reshape(depth, 1, -1))


def _inproj_kernel(x_ref, mod_ref, g_ref, w_ref, o_ref):
    h = _norm_mod(x_ref[0], g_ref[...], mod_ref[0, 0:1, :], mod_ref[0, 1:2, :])
    o_ref[0] = _dot(h.astype(BF16), w_ref[...])


def _inproj(x, mod, g, w, tile):
    bn, ln, _ = x.shape
    n = w.shape[1]
    per_batch = mod.shape[0] > 1
    return pl.pallas_call(
        _inproj_kernel,
        grid=(bn, ln // tile),
        in_specs=[
            pl.BlockSpec((1, tile, D_MODEL), lambda b, t: (b, t, 0)),
            pl.BlockSpec((1, 6, D_MODEL), (lambda b, t: (b, 0, 0)) if per_batch else (lambda b, t: (0, 0, 0))),
            pl.BlockSpec((1, D_MODEL), lambda b, t: (0, 0)),
            pl.BlockSpec((D_MODEL, n), lambda b, t: (0, 0)),
        ],
        out_specs=pl.BlockSpec((1, tile, n), lambda b, t: (b, t, 0)),
        out_shape=jax.ShapeDtypeStruct((bn, ln, n), F32),
        compiler_params=_cparams("parallel", "parallel"),
        name="inproj",
    )(x, mod, g, w)


def _dft_kernel(a_ref, cs_ref, m_ref, o_ref, z_ref, *, seq, scale):
    @pl.when(pl.program_id(1) == 0)
    def _():
        for g in range(A_GROUPS):
            cols = slice(g * A_GROUP_DIM, (g + 1) * A_GROUP_DIM)
            xg = _dot(a_ref[0, :, cols].astype(BF16), cs_ref[...])
            z_ref[0:seq, cols] = xg[:, :A_GROUP_DIM].astype(BF16)
            z_ref[seq:2 * seq, cols] = xg[:, A_GROUP_DIM:].astype(BF16)

    o_ref[0] = (_dot(m_ref[...], z_ref[...]) * scale).astype(o_ref.dtype)


def _dft_matrices(seq):
    def cos_sin(n):
        idx = jnp.arange(n, dtype=jnp.int32)
        ang = ((idx[:, None] * idx[None, :]) % n).astype(F32) * (2.0 * jnp.pi / n)
        return jnp.cos(ang), jnp.sin(ang)
    cc, sc = cos_sin(A_GROUP_DIM)
    cl, sl = cos_sin(seq)
    return (jnp.concatenate([cc, sc], axis=1).astype(BF16),
            jnp.concatenate([cl, -sl], axis=1).astype(BF16))


def _dft(p, seq_block):
    bn, ln, _ = p.shape
    cs, m = _dft_matrices(ln)
    scale = float(1.0 / (ln * A_GROUP_DIM) ** 0.5)
    return pl.pallas_call(
        functools.partial(_dft_kernel, seq=ln, scale=scale),
        grid=(bn, ln // seq_block),
        in_specs=[
            pl.BlockSpec((1, ln, D_A), lambda b, k: (b, 0, 0)),
            pl.BlockSpec((A_GROUP_DIM, 2 * A_GROUP_DIM), lambda b, k: (0, 0)),
            pl.BlockSpec((seq_block, 2 * ln), lambda b, k: (k, 0)),
        ],
        out_specs=pl.BlockSpec((1, seq_block, D_A), lambda b, k: (b, k, 0)),
        out_shape=jax.ShapeDtypeStruct((bn, ln, D_A), BF16),
        scratch_shapes=[pltpu.VMEM((2 * ln, D_A), BF16)],
        compiler_params=_cparams("parallel", "arbitrary"),
        name="dft",
    )(p, cs, m)


def _convmod_kernel(u_ref, g_ref, cw_ref, lg_ref, lb_ref, o_ref, bp_ref, *, seq, rows):
    zeros = jnp.zeros((CONV_PAD, D_B), F32)
    bp_ref[0:CONV_PAD, :] = zeros
    bp_ref[seq + CONV_PAD:seq + 2 * CONV_PAD, :] = zeros
    bp_ref[CONV_PAD:seq + CONV_PAD, :] = u_ref[0] * _sigmoid(g_ref[0])

    def body(i, carry):
        r0 = pl.multiple_of(i * rows, rows)
        win = bp_ref[pl.ds(r0, rows + 2 * CONV_PAD), :]
        acc = jnp.zeros((rows, D_B), F32)
        for k in range(CONV_WIDTH):
            off = k + CONV_PAD - CONV_WIDTH // 2
            acc = acc + win[off:off + rows, :] * cw_ref[k:k + 1, :]
        mu = jnp.mean(acc, axis=-1, keepdims=True)
        cen = acc - mu
        var = jnp.mean(cen * cen, axis=-1, keepdims=True)
        y = cen * lax.rsqrt(var + EPS) * lg_ref[...] + lb_ref[...]
        o_ref[0, pl.ds(r0, rows), :] = _silu(y).astype(o_ref.dtype)
        return carry

    lax.fori_loop(0, seq // rows, body, 0)


def _convmod(p, conv_w, ln_g, ln_b):
    bn, ln, _ = p.shape
    cw = jnp.pad(conv_w, ((0, 32 - CONV_WIDTH), (0, 0)))
    return pl.pallas_call(
        functools.partial(_convmod_kernel, seq=ln, rows=64),
        grid=(bn,),
        in_specs=[
            pl.BlockSpec((1, ln, D_B), lambda b: (b, 0, D_A // D_B)),
            pl.BlockSpec((1, ln, D_B), lambda b: (b, 0, D_A // D_B + 1)),
            pl.BlockSpec((32, D_B), lambda b: (0, 0)),
            pl.BlockSpec((1, D_B), lambda b: (0, 0)),
            pl.BlockSpec((1, D_B), lambda b: (0, 0)),
        ],
        out_specs=pl.BlockSpec((1, ln, D_B), lambda b: (b, 0, 0)),
        out_shape=jax.ShapeDtypeStruct((bn, ln, D_B), BF16),
        scratch_shapes=[pltpu.VMEM((ln + 2 * CONV_PAD, D_B), F32)],
        compiler_params=_cparams("parallel"),
        name="convmod",
    )(p, p, cw, ln_g.reshape(1, -1), ln_b.reshape(1, -1))


def _outproj_kernel(x_ref, y1_ref, y2_ref, mod_ref, w1_ref, w2_ref, o_ref):
    y = _dot(y1_ref[0], w1_ref[...]) + _dot(y2_ref[0], w2_ref[...])
    o_ref[0] = x_ref[0] + mod_ref[0, 2:3, :] * y


def _outproj(x, y1, y2, mod, w1, w2, tile):
    bn, ln, _ = x.shape
    d1, d2 = y1.shape[2], y2.shape[2]
    per_batch = mod.shape[0] > 1
    return pl.pallas_call(
        _outproj_kernel,
        grid=(bn, ln // tile),
        in_specs=[
            pl.BlockSpec((1, tile, D_MODEL), lambda b, t: (b, t, 0)),
            pl.BlockSpec((1, tile, d1), lambda b, t: (b, t, 0)),
            pl.BlockSpec((1, tile, d2), lambda b, t: (b, t, 0)),
            pl.BlockSpec((1, 6, D_MODEL), (lambda b, t: (b, 0, 0)) if per_batch else (lambda b, t: (0, 0, 0))),
            pl.BlockSpec((d1, D_MODEL), lambda b, t: (0, 0)),
            pl.BlockSpec((d2, D_MODEL), lambda b, t: (0, 0)),
        ],
        out_specs=pl.BlockSpec((1, tile, D_MODEL), lambda b, t: (b, t, 0)),
        out_shape=jax.ShapeDtypeStruct((bn, ln, D_MODEL), F32),
        compiler_params=_cparams("parallel", "parallel"),
        name="outproj",
    )(x, y1, y2, mod, w1, w2)


def _ffn_kernel(xp_ref, x_ref, xn_ref, mod_ref, g_ref, wg_ref, wv_ref, cg_ref, cv_ref, wd_ref,
                o_ref, h_ref, acc_ref, *, tile):
    t = pl.program_id(1)
    last = pl.num_programs(1) - 1
    g = g_ref[...]
    shift, scale, gate = mod_ref[0, 3:4, :], mod_ref[0, 4:5, :], mod_ref[0, 5:6, :]
    ext = tile + 2 * FFN_HALO

    hp = _norm_mod(xp_ref[0], g, shift, scale)
    hn = _norm_mod(xn_ref[0], g, shift, scale)
    h_ref[0:FFN_HALO, :] = jnp.where(t > 0, hp, 0.0).astype(BF16)
    h_ref[FFN_HALO:FFN_HALO + tile, :] = _norm_mod(x_ref[0], g, shift, scale).astype(BF16)
    h_ref[FFN_HALO + tile:ext, :] = jnp.where(t < last, hn, 0.0).astype(BF16)
    acc_ref[...] = jnp.zeros_like(acc_ref)

    def conv3(up, cw):
        prev = pltpu.roll(up, 1, 0)
        nxt = pltpu.roll(up, ext - 1, 0)
        out = prev * cw[0:1, :] + up * cw[1:2, :] + nxt * cw[2:3, :]
        return out[FFN_HALO:FFN_HALO + tile, :]

    def body(j, carry):
        hx = h_ref[...]
        ug = conv3(_dot(hx, wg_ref[j]), cg_ref[j])
        uv = conv3(_dot(hx, wv_ref[j]), cv_ref[j])
        act = _silu(ug) * uv
        acc_ref[...] += _dot(act.astype(BF16), wd_ref[j])
        return carry

    lax.fori_loop(0, D_FF // FFN_CHUNK, body, 0)
    o_ref[0] = x_ref[0] + gate * acc_ref[...]


def _ffn(x, mod, g, w_up, w_conv, w_down, tile):
    bn, ln, _ = x.shape
    nch = D_FF // FFN_CHUNK
    per_batch = mod.shape[0] > 1
    hb = tile // FFN_HALO
    nhb = ln // FFN_HALO
    wg = w_up[:, :D_FF].reshape(D_MODEL, nch, FFN_CHUNK).transpose(1, 0, 2).astype(BF16)
    wv = w_up[:, D_FF:].reshape(D_MODEL, nch, FFN_CHUNK).transpose(1, 0, 2).astype(BF16)
    wc = jnp.pad(w_conv, ((0, 5), (0, 0)))
    cg = wc[:, :D_FF].reshape(8, nch, FFN_CHUNK).transpose(1, 0, 2)
    cv = wc[:, D_FF:].reshape(8, nch, FFN_CHUNK).transpose(1, 0, 2)
    wd = w_down.reshape(nch, FFN_CHUNK, D_MODEL).astype(BF16)
    const3 = lambda b, t: (0, 0, 0)
    return pl.pallas_call(
        functools.partial(_ffn_kernel, tile=tile),
        grid=(bn, ln // tile),
        in_specs=[
            pl.BlockSpec((1, FFN_HALO, D_MODEL), lambda b, t: (b, jnp.maximum(t * hb - 1, 0), 0)),
            pl.BlockSpec((1, tile, D_MODEL), lambda b, t: (b, t, 0)),
            pl.BlockSpec((1, FFN_HALO, D_MODEL), lambda b, t: (b, jnp.minimum((t + 1) * hb, nhb - 1), 0)),
            pl.BlockSpec((1, 6, D_MODEL), (lambda b, t: (b, 0, 0)) if per_batch else const3),
            pl.BlockSpec((1, D_MODEL), lambda b, t: (0, 0)),
            pl.BlockSpec((nch, D_MODEL, FFN_CHUNK), const3),
            pl.BlockSpec((nch, D_MODEL, FFN_CHUNK), const3),
            pl.BlockSpec((nch, 8, FFN_CHUNK), const3),
            pl.BlockSpec((nch, 8, FFN_CHUNK), const3),
            pl.BlockSpec((nch, FFN_CHUNK, D_MODEL), const3),
        ],
        out_specs=pl.BlockSpec((1, tile, D_MODEL), lambda b, t: (b, t, 0)),
        out_shape=jax.ShapeDtypeStruct((bn, ln, D_MODEL), F32),
        scratch_shapes=[pltpu.VMEM((tile + 2 * FFN_HALO, D_MODEL), BF16),
                        pltpu.VMEM((tile, D_MODEL), F32)],
        compiler_params=_cparams("parallel", "parallel"),
        name="ffn",
    )(x, x, x, mod, g, wg, wv, cg, cv, wd)


def _head_norm(xb, ones_ref, gain):
    sq = xb * xb
    hi = sq.astype(BF16)
    lo = (sq - hi.astype(F32)).astype(BF16)
    ss = _dot(hi, ones_ref[...]) + _dot(lo, ones_ref[...])
    return xb * lax.rsqrt(ss * (1.0 / HEAD_DIM) + EPS) * gain


def _rope(y, cos, sin, first):
    partner = jnp.where(first, pltpu.roll(y, LANES - ROPE_PAIRS, 1), pltpu.roll(y, ROPE_PAIRS, 1))
    return y * cos + partner * sin


def _qkprep_kernel(p_ref, cos_ref, sin_ref, gq_ref, gk_ref, ones_ref, q_ref, k_ref, v_ref):
    lane = lax.broadcasted_iota(jnp.int32, (1, LANES), 1)
    first = (lane % (2 * ROPE_PAIRS)) < ROPE_PAIRS
    cos, sin = cos_ref[...], sin_ref[...]
    for c in range(D_Q // LANES):
        cols = slice(c * LANES, (c + 1) * LANES)
        y = _head_norm(p_ref[0, :, cols], ones_ref, gq_ref[...])
        q_ref[0, :, cols] = (_rope(y, cos, sin, first) * (HEAD_DIM ** -0.5)).astype(BF16)
    for c in range(D_KV // LANES):
        src = slice(D_Q + c * LANES, D_Q + (c + 1) * LANES)
        y = _head_norm(p_ref[0, :, src], ones_ref, gk_ref[...])
        k_ref[0, :, c * LANES:(c + 1) * LANES] = _rope(y, cos, sin, first).astype(BF16)
    v_ref[0] = p_ref[0, :, D_Q + D_KV:D_Q + 2 * D_KV].astype(BF16)


def _kvprep_ctx_kernel(p_ref, gk_ref, ones_ref, k_ref, v_ref):
    for c in range(D_KV // LANES):
        cols = slice(c * LANES, (c + 1) * LANES)
        k_ref[0, :, cols] = _head_norm(p_ref[0, :, cols], ones_ref, gk_ref[...]).astype(BF16)
    v_ref[0] = p_ref[0, :, D_KV:2 * D_KV].astype(BF16)


def _group_ones():
    r = jnp.arange(LANES) // HEAD_DIM
    return (r[:, None] == r[None, :]).astype(BF16)


def _rope_tables(seq):
    t = jnp.arange(seq)
    freqs = ROPE_THETA ** (-jnp.arange(ROPE_PAIRS, dtype=F32) / ROPE_PAIRS)
    ang_r = (t // GRID_W).astype(F32)[:, None] * freqs
    ang_c = (t % GRID_W).astype(F32)[:, None] * freqs
    cos = jnp.concatenate([jnp.cos(ang_r)] * 2 + [jnp.cos(ang_c)] * 2, axis=1)
    sin = jnp.concatenate([-jnp.sin(ang_r), jnp.sin(ang_r), -jnp.sin(ang_c), jnp.sin(ang_c)], axis=1)
    return jnp.tile(cos, (1, LANES // HEAD_DIM)), jnp.tile(sin, (1, LANES // HEAD_DIM))


def _qkprep(p, q_g, k_g, tile):
    bn, ln, n = p.shape
    cos, sin = _rope_tables(ln)
    gq = jnp.tile(q_g, LANES // HEAD_DIM).reshape(1, LANES)
    gk = jnp.tile(k_g, LANES // HEAD_DIM).reshape(1, LANES)
    row = lambda b, t: (b, t, 0)
    tab = lambda b, t: (t, 0)
    const = lambda b, t: (0, 0)
    return pl.pallas_call(
        _qkprep_kernel,
        grid=(bn, ln // tile),
        in_specs=[
            pl.BlockSpec((1, tile, n), row),
            pl.BlockSpec((tile, LANES), tab),
            pl.BlockSpec((tile, LANES), tab),
            pl.BlockSpec((1, LANES), const),
            pl.BlockSpec((1, LANES), const),
            pl.BlockSpec((LANES, LANES), const),
        ],
        out_specs=[pl.BlockSpec((1, tile, D_Q), row),
                   pl.BlockSpec((1, tile, D_KV), row),
                   pl.BlockSpec((1, tile, D_KV), row)],
        out_shape=[jax.ShapeDtypeStruct((bn, ln, D_Q), BF16),
                   jax.ShapeDtypeStruct((bn, ln, D_KV), BF16),
                   jax.ShapeDtypeStruct((bn, ln, D_KV), BF16)],
        compiler_params=_cparams("parallel", "parallel"),
        name="qkprep",
    )(p, cos, sin, gq, gk, _group_ones())


def _kvprep_ctx(pc, k_g):
    bn, lc, n = pc.shape
    gk = jnp.tile(k_g, LANES // HEAD_DIM).reshape(1, LANES)
    return pl.pallas_call(
        _kvprep_ctx_kernel,
        grid=(bn,),
        in_specs=[
            pl.BlockSpec((1, lc, n), lambda b: (b, 0, 0)),
            pl.BlockSpec((1, LANES), lambda b: (0, 0)),
            pl.BlockSpec((LANES, LANES), lambda b: (0, 0)),
        ],
        out_specs=[pl.BlockSpec((1, lc, D_KV), lambda b: (b, 0, 0)),
                   pl.BlockSpec((1, lc, D_KV), lambda b: (b, 0, 0))],
        out_shape=[jax.ShapeDtypeStruct((bn, lc, D_KV), BF16),
                   jax.ShapeDtypeStruct((bn, lc, D_KV), BF16)],
        compiler_params=_cparams("parallel"),
        name="kvprep_ctx",
    )(pc, gk, _group_ones())


def _attn_kernel(q_ref, k_ref, v_ref, o_ref, vlo_ref, vhi_ref):
    low = lax.broadcasted_iota(jnp.int32, (1, LANES), 1) < HEAD_DIM

    @pl.when(pl.program_id(2) == 0)
    def _():
        v = v_ref[0]
        zero = jnp.zeros_like(v)
        vlo_ref[...] = jnp.where(low, v, zero)
        vhi_ref[...] = jnp.where(low, zero, v)

    k = k_ref[0]
    contract_lanes = (((1,), (1,)), ((), ()))
    for c in range(q_ref.shape[2] // LANES):
        cols = slice(c * LANES, (c + 1) * LANES)
        qp = q_ref[0, :, cols]
        zero = jnp.zeros_like(qp)
        out = None
        for qm, vm_ref in ((jnp.where(low, qp, zero), vlo_ref), (jnp.where(low, zero, qp), vhi_ref)):
            s = lax.dot_general(qm, k, contract_lanes, preferred_element_type=F32)
            m = jnp.max(s, axis=-1, keepdims=True)
            e = jnp.exp(s - m)
            l = jnp.sum(e, axis=-1, keepdims=True)
            o = _dot(e.astype(BF16), vm_ref[...]) / l
            out = o if out is None else out + o
        o_ref[0, :, cols] = out.astype(o_ref.dtype)


def _attention(q, k_all, v_all, tile):
    bn, ln, _ = q.shape
    lk = k_all.shape[1]
    qw = D_Q // 2
    kw = D_KV // 2
    return pl.pallas_call(
        _attn_kernel,
        grid=(bn, 2, ln // tile),
        in_specs=[
            pl.BlockSpec((1, tile, qw), lambda b, j, t: (b, t, j)),
            pl.BlockSpec((1, lk, kw), lambda b, j, t: (b, 0, j)),
            pl.BlockSpec((1, lk, kw), lambda b, j, t: (b, 0, j)),
        ],
        out_specs=pl.BlockSpec((1, tile, qw), lambda b, j, t: (b, t, j)),
        out_shape=jax.ShapeDtypeStruct((bn, ln, D_Q), BF16),
        scratch_shapes=[pltpu.VMEM((lk, kw), BF16), pltpu.VMEM((lk, kw), BF16)],
        compiler_params=_cparams("parallel", "parallel", "arbitrary"),
        name="attention",
    )(q, k_all, v_all)


def _pool_kernel(u_ref, cnt_ref, w_ref, sc_ref, o_ref, pad_ref, *, seq):
    n = seq + 2 * POOL_PAD
    zeros = jnp.zeros((POOL_PAD, D_POOL), F32)
    pad_ref[0:POOL_PAD, :] = zeros
    pad_ref[seq + POOL_PAD:n, :] = zeros
    pad_ref[POOL_PAD:seq + POOL_PAD, :] = u_ref[0]
    a = pad_ref[...]
    w2 = a + pltpu.roll(a, 1, 0)
    w4 = pltpu.roll(w2, 1, 0) + pltpu.roll(w2, n - 1, 0)
    w8 = pltpu.roll(w4, 2, 0) + pltpu.roll(w4, n - 2, 0)
    w16 = pltpu.roll(w8, 4, 0) + pltpu.roll(w8, n - 4, 0)
    lane = lax.broadcasted_iota(jnp.int32, (1, D_POOL), 1)
    g = lane // POOL_GROUP_DIM
    win = jnp.where(g == 0, w2, jnp.where(g == 1, w4, jnp.where(g == 2, w8, w16)))
    u = u_ref[0]
    pooled = win[POOL_PAD:seq + POOL_PAD, :] / cnt_ref[...] - u
    o_ref[0] = (_dot(pooled.astype(BF16), w_ref[...]) * sc_ref[...]).astype(o_ref.dtype)


def _pool(p, pool_w, pool_scale):
    bn, ln, n = p.shape
    t = jnp.arange(ln)
    cnt = jnp.concatenate(
        [jnp.broadcast_to((jnp.minimum(t + w // 2, ln) - jnp.maximum(t - w // 2, 0)).astype(F32)[:, None],
                          (ln, POOL_GROUP_DIM)) for w in POOL_WINDOWS], axis=1)
    wbd = jax.scipy.linalg.block_diag(*[pool_w[i] for i in range(pool_w.shape[0])]).astype(BF16)
    return pl.pallas_call(
        functools.partial(_pool_kernel, seq=ln),
        grid=(bn,),
        in_specs=[
            pl.BlockSpec((1, ln, D_POOL), lambda b: (b, 0, n // D_POOL - 1)),
            pl.BlockSpec((ln, D_POOL), lambda b: (0, 0)),
            pl.BlockSpec((D_POOL, D_POOL), lambda b: (0, 0)),
            pl.BlockSpec((1, D_POOL), lambda b: (0, 0)),
        ],
        out_specs=pl.BlockSpec((1, ln, D_POOL), lambda b: (b, 0, 0)),
        out_shape=jax.ShapeDtypeStruct((bn, ln, D_POOL), BF16),
        scratch_shapes=[pltpu.VMEM((ln + 2 * POOL_PAD, D_POOL), F32)],
        compiler_params=_cparams("parallel"),
        name="pool",
    )(p, cnt, wbd, pool_scale.reshape(1, -1))


def _paired_head_order():
    order = []
    per_kv = N_Q_HEADS // N_KV_HEADS
    for j in range(N_KV_HEADS // 2):
        for i in range(per_kv):
            order += [(2 * j) * per_kv + i, (2 * j + 1) * per_kv + i]
    return order


def _even_layer(x, mod, norm1_g, norm2_g, w_in, conv_w, ln_g, ln_b, w_out, w_up, w_conv, w_down, tile):
    p = _inproj(x, mod, norm1_g.reshape(1, -1), w_in.astype(BF16), tile)
    fa = _dft(p, min(512, p.shape[1]))
    bb = _convmod(p, conv_w, ln_g, ln_b)
    wo = w_out.astype(BF16)
    x = _outproj(x, fa, bb, mod, wo[:D_A], wo[D_A:], tile)
    return _ffn(x, mod, norm2_g.reshape(1, -1), w_up, w_conv, w_down, tile)


def _odd_layer_last(x, ctx, mod, modc, norm1_g, norm2_g, w_in, q_g, k_g, pool_w, pool_scale, w_out,
                    w_up, w_conv, w_down, tile):
    heads = jnp.asarray(_paired_head_order())
    qcols = (heads[:, None] * HEAD_DIM + jnp.arange(HEAD_DIM)[None, :]).reshape(-1)
    w_in_b = w_in.astype(BF16)
    w_lat = jnp.concatenate([w_in_b[:, qcols], w_in_b[:, D_Q:]], axis=1)
    g1 = norm1_g.reshape(1, -1)
    p = _inproj(x, mod, g1, w_lat, tile)
    q, k, v = _qkprep(p, q_g, k_g, tile)
    pc = _inproj(ctx, modc, g1, w_in_b[:, D_Q:D_Q + 2 * D_KV], ctx.shape[1])
    kc, vc = _kvprep_ctx(pc, k_g)
    attn = _attention(q, jnp.concatenate([kc, k], axis=1), jnp.concatenate([vc, v], axis=1), 256)
    pooled = _pool(p, pool_w, pool_scale)
    wo = w_out.astype(BF16)
    x = _outproj(x, attn, pooled, mod, wo[:D_Q][qcols], wo[D_Q:], tile)
    return _ffn(x, mod, norm2_g.reshape(1, -1), w_up, w_conv, w_down, tile)


def kernel(x, c, ctx, c_ctx, w_ada, b_ada, norm1_g, norm2_g, ev_w_in, ev_conv_w, ev_ln_g, ev_ln_b, ev_w_out,
           od_w_in, od_q_g, od_k_g, od_pool_w, od_pool_scale, od_w_out, ffn_w_up, ffn_conv_w, ffn_w_down):
    depth = w_ada.shape[0]
    assert depth == 2, "even layer followed by a final odd layer"
    bn = x.shape[0]
    rows = -(-(bn + 1) // 8) * 8
    cc = jnp.concatenate([c, c_ctx[None, :], jnp.zeros((rows - bn - 1, D_MODEL), F32)], axis=0)
    mods = _ada(cc, w_ada, b_ada)
    mod = [mods[i, :bn].reshape(bn, 6, D_MODEL) for i in range(depth)]
    modc = [mods[i, bn:bn + 1].reshape(1, 6, D_MODEL) for i in range(depth)]

    tile = 512
    ev = (ev_w_in[0], ev_conv_w[0], ev_ln_g[0], ev_ln_b[0], ev_w_out[0],
          ffn_w_up[0], ffn_conv_w[0], ffn_w_down[0])
    x = _even_layer(x, mod[0], norm1_g[0], norm2_g[0], *ev, tile)
    ctx = _even_layer(ctx, modc[0], norm1_g[0], norm2_g[0], *ev, ctx.shape[1])
    return _odd_layer_last(x, ctx, mod[1], modc[1], norm1_g[1], norm2_g[1], od_w_in[0], od_q_g[0], od_k_g[0],
                           od_pool_w[0], od_pool_scale[0], od_w_out[0],
                           ffn_w_up[1], ffn_conv_w[1], ffn_w_down[1], tile)
```

```python
import functools

import jax
import jax.numpy as jnp
from jax import lax
from jax.experimental import pallas as pl
from jax.experimental.pallas import tpu as pltpu

F32 = jnp.float32
BF16 = jnp.bfloat16

D_MODEL = 1024
GRID_W = 64
EPS = 1e-6

A_GROUPS = 4
A_GROUP_DIM = 128
D_A = A_GROUPS * A_GROUP_DIM
D_B = 512
CONV_WIDTH = 31
CONV_PAD = 16

HEAD_DIM = 64
N_Q_HEADS = 12
N_KV_HEADS = 4
D_Q = N_Q_HEADS * HEAD_DIM
D_KV = N_KV_HEADS * HEAD_DIM
POOL_WINDOWS = (2, 4, 8, 16)
POOL_GROUP_DIM = 64
D_POOL = len(POOL_WINDOWS) * POOL_GROUP_DIM
POOL_PAD = 16
ROPE_THETA = 10000.0
ROPE_PAIRS = HEAD_DIM // 4

D_FF = 2816
FFN_CHUNK = 256
FFN_HALO = 16

LANES = 128
VMEM_LIMIT = 56 * 1024 * 1024


def _cparams(*sem):
    return pltpu.CompilerParams(dimension_semantics=sem, vmem_limit_bytes=VMEM_LIMIT)


def _sigmoid(x):
    return 1.0 / (1.0 + jnp.exp(-x))


def _silu(x):
    return x * _sigmoid(x)


def _norm_mod(x, g, shift, scale):
    ms = jnp.mean(x * x, axis=-1, keepdims=True)
    y = x * lax.rsqrt(ms + EPS) * g
    return y * (1.0 + scale) + shift


def _dot(a, b):
    return jnp.dot(a, b, preferred_element_type=F32)


def _ada_kernel(cc_ref, w_ref, b_ref, o_ref):
    s = _silu(cc_ref[...])
    o_ref[0] = _dot(s.astype(BF16), w_ref[0].astype(BF16)) + b_ref[0]


def _ada(cc, w_ada, b_ada):
    depth = w_ada.shape[0]
    rows = cc.shape[0]
    nblk = w_ada.shape[2] // D_MODEL
    return pl.pallas_call(
        _ada_kernel,
        grid=(depth, nblk),
        in_specs=[
            pl.BlockSpec((rows, D_MODEL), lambda i, n: (0, 0)),
            pl.BlockSpec((1, D_MODEL, D_MODEL), lambda i, n: (i, 0, n)),
            pl.BlockSpec((1, 1, D_MODEL), lambda i, n: (i, 0, n)),
        ],
        out_specs=pl.BlockSpec((1, rows, D_MODEL), lambda i, n: (i, 0, n)),
        out_shape=jax.ShapeDtypeStruct((depth, rows, w_ada.shape[2]), F32),
        compiler_params=_cparams("parallel", "parallel"),
        name="ada",
    )(cc, w_ada, b_ada.reshape(depth, 1, -1))


def _inproj_kernel(x_ref, mod_ref, g_ref, w_ref, o_ref):
    h = _norm_mod(x_ref[0], g_ref[...], mod_ref[0, 0:1, :], mod_ref[0, 1:2, :])
    o_ref[0] = _dot(h.astype(BF16), w_ref[...])


def _inproj(x, mod, g, w, tile):
    bn, ln, _ = x.shape
    n = w.shape[1]
    per_batch = mod.shape[0] > 1
    return pl.pallas_call(
        _inproj_kernel,
        grid=(bn, ln // tile),
        in_specs=[
            pl.BlockSpec((1, tile, D_MODEL), lambda b, t: (b, t, 0)),
            pl.BlockSpec((1, 6, D_MODEL), (lambda b, t: (b, 0, 0)) if per_batch else (lambda b, t: (0, 0, 0))),
            pl.BlockSpec((1, D_MODEL), lambda b, t: (0, 0)),
            pl.BlockSpec((D_MODEL, n), lambda b, t: (0, 0)),
        ],
        out_specs=pl.BlockSpec((1, tile, n), lambda b, t: (b, t, 0)),
        out_shape=jax.ShapeDtypeStruct((bn, ln, n), F32),
        compiler_params=_cparams("parallel", "parallel"),
        name="inproj",
    )(x, mod, g, w)


def _dft_kernel(a_ref, cs_ref, m_ref, o_ref, z_ref, *, seq, scale):
    @pl.when(pl.program_id(1) == 0)
    def _():
        for g in range(A_GROUPS):
            cols = slice(g * A_GROUP_DIM, (g + 1) * A_GROUP_DIM)
            xg = _dot(a_ref[0, :, cols].astype(BF16), cs_ref[...])
            z_ref[0:seq, cols] = xg[:, :A_GROUP_DIM].astype(BF16)
            z_ref[seq:2 * seq, cols] = xg[:, A_GROUP_DIM:].astype(BF16)

    o_ref[0] = (_dot(m_ref[...], z_ref[...]) * scale).astype(o_ref.dtype)


def _dft_matrices(seq):
    def cos_sin(n):
        idx = jnp.arange(n, dtype=jnp.int32)
        ang = ((idx[:, None] * idx[None, :]) % n).astype(F32) * (2.0 * jnp.pi / n)
        return jnp.cos(ang), jnp.sin(ang)
    cc, sc = cos_sin(A_GROUP_DIM)
    cl, sl = cos_sin(seq)
    return (jnp.concatenate([cc, sc], axis=1).astype(BF16),
            jnp.concatenate([cl, -sl], axis=1).astype(BF16))


def _dft(p, seq_block):
    bn, ln, _ = p.shape
    cs, m = _dft_matrices(ln)
    scale = float(1.0 / (ln * A_GROUP_DIM) ** 0.5)
    return pl.pallas_call(
        functools.partial(_dft_kernel, seq=ln, scale=scale),
        grid=(bn, ln // seq_block),
        in_specs=[
            pl.BlockSpec((1, ln, D_A), lambda b, k: (b, 0, 0)),
            pl.BlockSpec((A_GROUP_DIM, 2 * A_GROUP_DIM), lambda b, k: (0, 0)),
            pl.BlockSpec((seq_block, 2 * ln), lambda b, k: (k, 0)),
        ],
        out_specs=pl.BlockSpec((1, seq_block, D_A), lambda b, k: (b, k, 0)),
        out_shape=jax.ShapeDtypeStruct((bn, ln, D_A), BF16),
        scratch_shapes=[pltpu.VMEM((2 * ln, D_A), BF16)],
        compiler_params=_cparams("parallel", "arbitrary"),
        name="dft",
    )(p, cs, m)


def _convmod_kernel(u_ref, g_ref, cw_ref, lg_ref, lb_ref, o_ref, bp_ref, *, seq, rows):
    zeros = jnp.zeros((CONV_PAD, D_B), F32)
    bp_ref[0:CONV_PAD, :] = zeros
    bp_ref[seq + CONV_PAD:seq + 2 * CONV_PAD, :] = zeros
    bp_ref[CONV_PAD:seq + CONV_PAD, :] = u_ref[0] * _sigmoid(g_ref[0])

    def body(i, carry):
        r0 = pl.multiple_of(i * rows, rows)
        win = bp_ref[pl.ds(r0, rows + 2 * CONV_PAD), :]
        acc = jnp.zeros((rows, D_B), F32)
        for k in range(CONV_WIDTH):
            off = k + CONV_PAD - CONV_WIDTH // 2
            acc = acc + win[off:off + rows, :] * cw_ref[k:k + 1, :]
        mu = jnp.mean(acc, axis=-1, keepdims=True)
        cen = acc - mu
        var = jnp.mean(cen * cen, axis=-1, keepdims=True)
        y = cen * lax.rsqrt(var + EPS) * lg_ref[...] + lb_ref[...]
        o_ref[0, pl.ds(r0, rows), :] = _silu(y).astype(o_ref.dtype)
        return carry

    lax.fori_loop(0, seq // rows, body, 0)


def _convmod(p, conv_w, ln_g, ln_b):
    bn, ln, _ = p.shape
    cw = jnp.pad(conv_w, ((0, 32 - CONV_WIDTH), (0, 0)))
    return pl.pallas_call(
        functools.partial(_convmod_kernel, seq=ln, rows=64),
        grid=(bn,),
        in_specs=[
            pl.BlockSpec((1, ln, D_B), lambda b: (b, 0, D_A // D_B)),
            pl.BlockSpec((1, ln, D_B), lambda b: (b, 0, D_A // D_B + 1)),
            pl.BlockSpec((32, D_B), lambda b: (0, 0)),
            pl.BlockSpec((1, D_B), lambda b: (0, 0)),
            pl.BlockSpec((1, D_B), lambda b: (0, 0)),
        ],
        out_specs=pl.BlockSpec((1, ln, D_B), lambda b: (b, 0, 0)),
        out_shape=jax.ShapeDtypeStruct((bn, ln, D_B), BF16),
        scratch_shapes=[pltpu.VMEM((ln + 2 * CONV_PAD, D_B), F32)],
        compiler_params=_cparams("parallel"),
        name="convmod",
    )(p, p, cw, ln_g.reshape(1, -1), ln_b.reshape(1, -1))


def _outproj_kernel(x_ref, y1_ref, y2_ref, mod_ref, w1_ref, w2_ref, o_ref):
    y = _dot(y1_ref[0], w1_ref[...]) + _dot(y2_ref[0], w2_ref[...])
    o_ref[0] = x_ref[0] + mod_ref[0, 2:3, :] * y


def _outproj(x, y1, y2, mod, w1, w2, tile):
    bn, ln, _ = x.shape
    d1, d2 = y1.shape[2], y2.shape[2]
    per_batch = mod.shape[0] > 1
    return pl.pallas_call(
        _outproj_kernel,
        grid=(bn, ln // tile),
        in_specs=[
            pl.BlockSpec((1, tile, D_MODEL), lambda b, t: (b, t, 0)),
            pl.BlockSpec((1, tile, d1), lambda b, t: (b, t, 0)),
            pl.BlockSpec((1, tile, d2), lambda b, t: (b, t, 0)),
            pl.BlockSpec((1, 6, D_MODEL), (lambda b, t: (b, 0, 0)) if per_batch else (lambda b, t: (0, 0, 0))),
            pl.BlockSpec((d1, D_MODEL), lambda b, t: (0, 0)),
            pl.BlockSpec((d2, D_MODEL), lambda b, t: (0, 0)),
        ],
        out_specs=pl.BlockSpec((1, tile, D_MODEL), lambda b, t: (b, t, 0)),
        out_shape=jax.ShapeDtypeStruct((bn, ln, D_MODEL), F32),
        compiler_params=_cparams("parallel", "parallel"),
        name="outproj",
    )(x, y1, y2, mod, w1, w2)


def _ffn_kernel(xp_ref, x_ref, xn_ref, mod_ref, g_ref, wg_ref, wv_ref, cg_ref, cv_ref, wd_ref,
                o_ref, h_ref, acc_ref, *, tile):
    t = pl.program_id(1)
    last = pl.num_programs(1) - 1
    g = g_ref[...]
    shift, scale, gate = mod_ref[0, 3:4, :], mod_ref[0, 4:5, :], mod_ref[0, 5:6, :]
    ext = tile + 2 * FFN_HALO

    hp = _norm_mod(xp_ref[0], g, shift, scale)
    hn = _norm_mod(xn_ref[0], g, shift, scale)
    h_ref[0:FFN_HALO, :] = jnp.where(t > 0, hp, 0.0).astype(BF16)
    h_ref[FFN_HALO:FFN_HALO + tile, :] = _norm_mod(x_ref[0], g, shift, scale).astype(BF16)
    h_ref[FFN_HALO + tile:ext, :] = jnp.where(t < last, hn, 0.0).astype(BF16)
    acc_ref[...] = jnp.zeros_like(acc_ref)

    def conv3(up, cw):
        prev = pltpu.roll(up, 1, 0)
        nxt = pltpu.roll(up, ext - 1, 0)
        out = prev * cw[0:1, :] + up * cw[1:2, :] + nxt * cw[2:3, :]
        return out[FFN_HALO:FFN_HALO + tile, :]

    def up(j):
        hx = h_ref[...]
        return _dot(hx, wg_ref[j]), _dot(hx, wv_ref[j])

    nch = D_FF // FFN_CHUNK
    nxt = up(0)
    for j in range(nch):
        cur = nxt
        if j + 1 < nch:
            nxt = up(j + 1)
        act = _silu(conv3(cur[0], cg_ref[j])) * conv3(cur[1], cv_ref[j])
        acc_ref[...] += _dot(act.astype(BF16), wd_ref[j])
    o_ref[0] = x_ref[0] + gate * acc_ref[...]


def _ffn(x, mod, g, w_up, w_conv, w_down, tile):
    bn, ln, _ = x.shape
    nch = D_FF // FFN_CHUNK
    per_batch = mod.shape[0] > 1
    hb = tile // FFN_HALO
    nhb = ln // FFN_HALO
    wg = w_up[:, :D_FF].reshape(D_MODEL, nch, FFN_CHUNK).transpose(1, 0, 2).astype(BF16)
    wv = w_up[:, D_FF:].reshape(D_MODEL, nch, FFN_CHUNK).transpose(1, 0, 2).astype(BF16)
    wc = jnp.pad(w_conv, ((0, 5), (0, 0)))
    cg = wc[:, :D_FF].reshape(8, nch, FFN_CHUNK).transpose(1, 0, 2)
    cv = wc[:, D_FF:].reshape(8, nch, FFN_CHUNK).transpose(1, 0, 2)
    wd = w_down.reshape(nch, FFN_CHUNK, D_MODEL).astype(BF16)
    const3 = lambda b, t: (0, 0, 0)
    return pl.pallas_call(
        functools.partial(_ffn_kernel, tile=tile),
        grid=(bn, ln // tile),
        in_specs=[
            pl.BlockSpec((1, FFN_HALO, D_MODEL), lambda b, t: (b, jnp.maximum(t * hb - 1, 0), 0)),
            pl.BlockSpec((1, tile, D_MODEL), lambda b, t: (b, t, 0)),
            pl.BlockSpec((1, FFN_HALO, D_MODEL), lambda b, t: (b, jnp.minimum((t + 1) * hb, nhb - 1), 0)),
            pl.BlockSpec((1, 6, D_MODEL), (lambda b, t: (b, 0, 0)) if per_batch else const3),
            pl.BlockSpec((1, D_MODEL), lambda b, t: (0, 0)),
            pl.BlockSpec((nch, D_MODEL, FFN_CHUNK), const3),
            pl.BlockSpec((nch, D_MODEL, FFN_CHUNK), const3),
            pl.BlockSpec((nch, 8, FFN_CHUNK), const3),
            pl.BlockSpec((nch, 8, FFN_CHUNK), const3),
            pl.BlockSpec((nch, FFN_CHUNK, D_MODEL), const3),
        ],
        out_specs=pl.BlockSpec((1, tile, D_MODEL), lambda b, t: (b, t, 0)),
        out_shape=jax.ShapeDtypeStruct((bn, ln, D_MODEL), F32),
        scratch_shapes=[pltpu.VMEM((tile + 2 * FFN_HALO, D_MODEL), BF16),
                        pltpu.VMEM((tile, D_MODEL), F32)],
        compiler_params=_cparams("parallel", "parallel"),
        name="ffn",
    )(x, x, x, mod, g, wg, wv, cg, cv, wd)


def _head_norm(xb, ones_ref, gain):
    sq = xb * xb
    hi = sq.astype(BF16)
    lo = (sq - hi.astype(F32)).astype(BF16)
    ss = _dot(hi, ones_ref[...]) + _dot(lo, ones_ref[...])
    return xb * lax.rsqrt(ss * (1.0 / HEAD_DIM) + EPS) * gain


def _rope(y, cos, sin, first):
    partner = jnp.where(first, pltpu.roll(y, LANES - ROPE_PAIRS, 1), pltpu.roll(y, ROPE_PAIRS, 1))
    return y * cos + partner * sin


def _qkprep_kernel(p_ref, cos_ref, sin_ref, gq_ref, gk_ref, ones_ref, q_ref, k_ref, v_ref):
    lane = lax.broadcasted_iota(jnp.int32, (1, LANES), 1)
    first = (lane % (2 * ROPE_PAIRS)) < ROPE_PAIRS
    cos, sin = cos_ref[...], sin_ref[...]
    for c in range(D_Q // LANES):
        cols = slice(c * LANES, (c + 1) * LANES)
        y = _head_norm(p_ref[0, :, cols], ones_ref, gq_ref[...])
        q_ref[0, :, cols] = (_rope(y, cos, sin, first) * (HEAD_DIM ** -0.5)).astype(BF16)
    for c in range(D_KV // LANES):
        src = slice(D_Q + c * LANES, D_Q + (c + 1) * LANES)
        y = _head_norm(p_ref[0, :, src], ones_ref, gk_ref[...])
        k_ref[0, :, c * LANES:(c + 1) * LANES] = _rope(y, cos, sin, first).astype(BF16)
    v_ref[0] = p_ref[0, :, D_Q + D_KV:D_Q + 2 * D_KV].astype(BF16)


def _kvprep_ctx_kernel(p_ref, gk_ref, ones_ref, k_ref, v_ref):
    for c in range(D_KV // LANES):
        cols = slice(c * LANES, (c + 1) * LANES)
        k_ref[0, :, cols] = _head_norm(p_ref[0, :, cols], ones_ref, gk_ref[...]).astype(BF16)
    v_ref[0] = p_ref[0, :, D_KV:2 * D_KV].astype(BF16)


def _group_ones():
    r = jnp.arange(LANES) // HEAD_DIM
    return (r[:, None] == r[None, :]).astype(BF16)


def _rope_tables(seq):
    t = jnp.arange(seq)
    freqs = ROPE_THETA ** (-jnp.arange(ROPE_PAIRS, dtype=F32) / ROPE_PAIRS)
    ang_r = (t // GRID_W).astype(F32)[:, None] * freqs
    ang_c = (t % GRID_W).astype(F32)[:, None] * freqs
    cos = jnp.concatenate([jnp.cos(ang_r)] * 2 + [jnp.cos(ang_c)] * 2, axis=1)
    sin = jnp.concatenate([-jnp.sin(ang_r), jnp.sin(ang_r), -jnp.sin(ang_c), jnp.sin(ang_c)], axis=1)
    return jnp.tile(cos, (1, LANES // HEAD_DIM)), jnp.tile(sin, (1, LANES // HEAD_DIM))


def _qkprep(p, q_g, k_g, tile):
    bn, ln, n = p.shape
    cos, sin = _rope_tables(ln)
    gq = jnp.tile(q_g, LANES // HEAD_DIM).reshape(1, LANES)
    gk = jnp.tile(k_g, LANES // HEAD_DIM).reshape(1, LANES)
    row = lambda b, t: (b, t, 0)
    tab = lambda b, t: (t, 0)
    const = lambda b, t: (0, 0)
    return pl.pallas_call(
        _qkprep_kernel,
        grid=(bn, ln // tile),
        in_specs=[
            pl.BlockSpec((1, tile, n), row),
            pl.BlockSpec((tile, LANES), tab),
            pl.BlockSpec((tile, LANES), tab),
            pl.BlockSpec((1, LANES), const),
            pl.BlockSpec((1, LANES), const),
            pl.BlockSpec((LANES, LANES), const),
        ],
        out_specs=[pl.BlockSpec((1, tile, D_Q), row),
                   pl.BlockSpec((1, tile, D_KV), row),
                   pl.BlockSpec((1, tile, D_KV), row)],
        out_shape=[jax.ShapeDtypeStruct((bn, ln, D_Q), BF16),
                   jax.ShapeDtypeStruct((bn, ln, D_KV), BF16),
                   jax.ShapeDtypeStruct((bn, ln, D_KV), BF16)],
        compiler_params=_cparams("parallel", "parallel"),
        name="qkprep",
    )(p, cos, sin, gq, gk, _group_ones())


def _kvprep_ctx(pc, k_g):
    bn, lc, n = pc.shape
    gk = jnp.tile(k_g, LANES // HEAD_DIM).reshape(1, LANES)
    return pl.pallas_call(
        _kvprep_ctx_kernel,
        grid=(bn,),
        in_specs=[
            pl.BlockSpec((1, lc, n), lambda b: (b, 0, 0)),
            pl.BlockSpec((1, LANES), lambda b: (0, 0)),
            pl.BlockSpec((LANES, LANES), lambda b: (0, 0)),
        ],
        out_specs=[pl.BlockSpec((1, lc, D_KV), lambda b: (b, 0, 0)),
                   pl.BlockSpec((1, lc, D_KV), lambda b: (b, 0, 0))],
        out_shape=[jax.ShapeDtypeStruct((bn, lc, D_KV), BF16),
                   jax.ShapeDtypeStruct((bn, lc, D_KV), BF16)],
        compiler_params=_cparams("parallel"),
        name="kvprep_ctx",
    )(pc, gk, _group_ones())


def _attn_kernel(q_ref, k_ref, v_ref, o_ref, vlo_ref, vhi_ref):
    low = lax.broadcasted_iota(jnp.int32, (1, LANES), 1) < HEAD_DIM

    @pl.when(pl.program_id(2) == 0)
    def _():
        v = v_ref[0]
        zero = jnp.zeros_like(v)
        vlo_ref[...] = jnp.where(low, v, zero)
        vhi_ref[...] = jnp.where(low, zero, v)

    k = k_ref[0]
    contract_lanes = (((1,), (1,)), ((), ()))
    for c in range(q_ref.shape[2] // LANES):
        cols = slice(c * LANES, (c + 1) * LANES)
        qp = q_ref[0, :, cols]
        zero = jnp.zeros_like(qp)
        out = None
        for qm, vm_ref in ((jnp.where(low, qp, zero), vlo_ref), (jnp.where(low, zero, qp), vhi_ref)):
            s = lax.dot_general(qm, k, contract_lanes, preferred_element_type=F32)
            m = jnp.max(s, axis=-1, keepdims=True)
            e = jnp.exp(s - m)
            l = jnp.sum(e, axis=-1, keepdims=True)
            o = _dot(e.astype(BF16), vm_ref[...]) / l
            out = o if out is None else out + o
        o_ref[0, :, cols] = out.astype(o_ref.dtype)


def _attention(q, k_all, v_all, tile):
    bn, ln, _ = q.shape
    lk = k_all.shape[1]
    qw = D_Q // 2
    kw = D_KV // 2
    return pl.pallas_call(
        _attn_kernel,
        grid=(bn, 2, ln // tile),
        in_specs=[
            pl.BlockSpec((1, tile, qw), lambda b, j, t: (b, t, j)),
            pl.BlockSpec((1, lk, kw), lambda b, j, t: (b, 0, j)),
            pl.BlockSpec((1, lk, kw), lambda b, j, t: (b, 0, j)),
        ],
        out_specs=pl.BlockSpec((1, tile, qw), lambda b, j, t: (b, t, j)),
        out_shape=jax.ShapeDtypeStruct((bn, ln, D_Q), BF16),
        scratch_shapes=[pltpu.VMEM((lk, kw), BF16), pltpu.VMEM((lk, kw), BF16)],
        compiler_params=_cparams("parallel", "parallel", "arbitrary"),
        name="attention",
    )(q, k_all, v_all)


def _pool_kernel(u_ref, cnt_ref, w_ref, sc_ref, o_ref, pad_ref, *, seq):
    n = seq + 2 * POOL_PAD
    zeros = jnp.zeros((POOL_PAD, D_POOL), F32)
    pad_ref[0:POOL_PAD, :] = zeros
    pad_ref[seq + POOL_PAD:n, :] = zeros
    pad_ref[POOL_PAD:seq + POOL_PAD, :] = u_ref[0]
    a = pad_ref[...]
    w2 = a + pltpu.roll(a, 1, 0)
    w4 = pltpu.roll(w2, 1, 0) + pltpu.roll(w2, n - 1, 0)
    w8 = pltpu.roll(w4, 2, 0) + pltpu.roll(w4, n - 2, 0)
    w16 = pltpu.roll(w8, 4, 0) + pltpu.roll(w8, n - 4, 0)
    lane = lax.broadcasted_iota(jnp.int32, (1, D_POOL), 1)
    g = lane // POOL_GROUP_DIM
    win = jnp.where(g == 0, w2, jnp.where(g == 1, w4, jnp.where(g == 2, w8, w16)))
    u = u_ref[0]
    pooled = win[POOL_PAD:seq + POOL_PAD, :] / cnt_ref[...] - u
    o_ref[0] = (_dot(pooled.astype(BF16), w_ref[...]) * sc_ref[...]).astype(o_ref.dtype)


def _pool(p, pool_w, pool_scale):
    bn, ln, n = p.shape
    t = jnp.arange(ln)
    cnt = jnp.concatenate(
        [jnp.broadcast_to((jnp.minimum(t + w // 2, ln) - jnp.maximum(t - w // 2, 0)).astype(F32)[:, None],
                          (ln, POOL_GROUP_DIM)) for w in POOL_WINDOWS], axis=1)
    wbd = jax.scipy.linalg.block_diag(*[pool_w[i] for i in range(pool_w.shape[0])]).astype(BF16)
    return pl.pallas_call(
        functools.partial(_pool_kernel, seq=ln),
        grid=(bn,),
        in_specs=[
            pl.BlockSpec((1, ln, D_POOL), lambda b: (b, 0, n // D_POOL - 1)),
            pl.BlockSpec((ln, D_POOL), lambda b: (0, 0)),
            pl.BlockSpec((D_POOL, D_POOL), lambda b: (0, 0)),
            pl.BlockSpec((1, D_POOL), lambda b: (0, 0)),
        ],
        out_specs=pl.BlockSpec((1, ln, D_POOL), lambda b: (b, 0, 0)),
        out_shape=jax.ShapeDtypeStruct((bn, ln, D_POOL), BF16),
        scratch_shapes=[pltpu.VMEM((ln + 2 * POOL_PAD, D_POOL), F32)],
        compiler_params=_cparams("parallel"),
        name="pool",
    )(p, cnt, wbd, pool_scale.reshape(1, -1))


def _paired_head_order():
    order = []
    per_kv = N_Q_HEADS // N_KV_HEADS
    for j in range(N_KV_HEADS // 2):
        for i in range(per_kv):
            order += [(2 * j) * per_kv + i, (2 * j + 1) * per_kv + i]
    return order


def _even_layer(x, mod, norm1_g, norm2_g, w_in, conv_w, ln_g, ln_b, w_out, w_up, w_conv, w_down, tile):
    p = _inproj(x, mod, norm1_g.reshape(1, -1), w_in.astype(BF16), tile)
    fa = _dft(p, min(512, p.shape[1]))
    bb = _convmod(p, conv_w, ln_g, ln_b)
    wo = w_out.astype(BF16)
    x = _outproj(x, fa, bb, mod, wo[:D_A], wo[D_A:], tile)
    return _ffn(x, mod, norm2_g.reshape(1, -1), w_up, w_conv, w_down, tile)


def _odd_layer_last(x, ctx, mod, modc, norm1_g, norm2_g, w_in, q_g, k_g, pool_w, pool_scale, w_out,
                    w_up, w_conv, w_down, tile):
    heads = jnp.asarray(_paired_head_order())
    qcols = (heads[:, None] * HEAD_DIM + jnp.arange(HEAD_DIM)[None, :]).reshape(-1)
    w_in_b = w_in.astype(BF16)
    w_lat = jnp.concatenate([w_in_b[:, qcols], w_in_b[:, D_Q:]], axis=1)
    g1 = norm1_g.reshape(1, -1)
    p = _inproj(x, mod, g1, w_lat, tile)
    q, k, v = _qkprep(p, q_g, k_g, tile)
    pc = _inproj(ctx, modc, g1, w_in_b[:, D_Q:D_Q + 2 * D_KV], ctx.shape[1])
    kc, vc = _kvprep_ctx(pc, k_g)
    attn = _attention(q, jnp.concatenate([kc, k], axis=1), jnp.concatenate([vc, v], axis=1), 256)
    pooled = _pool(p, pool_w, pool_scale)
    wo = w_out.astype(BF16)
    x = _outproj(x, attn, pooled, mod, wo[:D_Q][qcols], wo[D_Q:], tile)
    return _ffn(x, mod, norm2_g.reshape(1, -1), w_up, w_conv, w_down, tile)


def kernel(x, c, ctx, c_ctx, w_ada, b_ada, norm1_g, norm2_g, ev_w_in, ev_conv_w, ev_ln_g, ev_ln_b, ev_w_out,
           od_w_in, od_q_g, od_k_g, od_pool_w, od_pool_scale, od_w_out, ffn_w_up, ffn_conv_w, ffn_w_down):
    depth = w_ada.shape[0]
    assert depth == 2, "even layer followed by a final odd layer"
    bn = x.shape[0]
    rows = -(-(bn + 1) // 8) * 8
    cc = jnp.concatenate([c, c_ctx[None, :], jnp.zeros((rows - bn - 1, D_MODEL), F32)], axis=0)
    mods = _ada(cc, w_ada, b_ada)
    mod = [mods[i, :bn].reshape(bn, 6, D_MODEL) for i in range(depth)]
    modc = [mods[i, bn:bn + 1].reshape(1, 6, D_MODEL) for i in range(depth)]

    tile = 512
    ev = (ev_w_in[0], ev_conv_w[0], ev_ln_g[0], ev_ln_b[0], ev_w_out[0],
          ffn_w_up[0], ffn_conv_w[0], ffn_w_down[0])
    x = _even_layer(x, mod[0], norm1_g[0], norm2_g[0], *ev, tile)
    ctx = _even_layer(ctx, modc[0], norm1_g[0], norm2_g[0], *ev, ctx.shape[1])
    return _odd_layer_last(x, ctx, mod[1], modc[1], norm1_g[1], norm2_g[1], od_w_in[0], od_q_g[0], od_k_g[0],
                           od_pool_w[0], od_pool_scale[0], od_w_out[0],
                           ffn_w_up[1], ffn_conv_w[1], ffn_w_down[1], tile)
```

```python
import functools

import jax
import jax.numpy as jnp
from jax import lax
from jax.experimental import pallas as pl
from jax.experimental.pallas import tpu as pltpu

F32 = jnp.float32
BF16 = jnp.bfloat16

D_MODEL = 1024
GRID_W = 64
EPS = 1e-6

A_GROUPS = 4
A_GROUP_DIM = 128
D_A = A_GROUPS * A_GROUP_DIM
D_B = 512
CONV_WIDTH = 31
CONV_PAD = 16
CONV_BLOCK = 8

HEAD_DIM = 64
N_Q_HEADS = 12
N_KV_HEADS = 4
D_Q = N_Q_HEADS * HEAD_DIM
D_KV = N_KV_HEADS * HEAD_DIM
POOL_WINDOWS = (2, 4, 8, 16)
POOL_GROUP_DIM = 64
D_POOL = len(POOL_WINDOWS) * POOL_GROUP_DIM
POOL_PAD = 16
ROPE_THETA = 10000.0
ROPE_PAIRS = HEAD_DIM // 4
Q_SCALE = HEAD_DIM ** -0.5 * 1.4426950408889634

D_FF = 2816
FFN_CHUNK = 256
ATTN_TILE = 1024
ATTN_SUB = 512
FFN_HALO = 16

LANES = 128
SUBLANES = 8
VMEM_LIMIT = 56 * 1024 * 1024


def _cparams(*sem):
    return pltpu.CompilerParams(dimension_semantics=sem, vmem_limit_bytes=VMEM_LIMIT)


def _sigmoid(x):
    return 1.0 / (1.0 + jnp.exp(-x))


def _silu(x):
    return x * _sigmoid(x)


def _norm_mod(x, g, shift, scale):
    ms = jnp.mean(x * x, axis=-1, keepdims=True)
    y = x * lax.rsqrt(ms + EPS) * g
    return y * (1.0 + scale) + shift


def _dot(a, b):
    return jnp.dot(a, b, preferred_element_type=F32)


def _ada_kernel(cc_ref, w_ref, b_ref, o_ref):
    s = _silu(cc_ref[...])
    o_ref[0] = _dot(s.astype(BF16), w_ref[0].astype(BF16)) + b_ref[0]


def _ada(cc, w_ada, b_ada):
    depth = w_ada.shape[0]
    rows = cc.shape[0]
    nblk = w_ada.shape[2] // D_MODEL
    return pl.pallas_call(
        _ada_kernel,
        grid=(depth, nblk),
        in_specs=[
            pl.BlockSpec((rows, D_MODEL), lambda i, n: (0, 0)),
            pl.BlockSpec((1, D_MODEL, D_MODEL), lambda i, n: (i, 0, n)),
            pl.BlockSpec((1, 1, D_MODEL), lambda i, n: (i, 0, n)),
        ],
        out_specs=pl.BlockSpec((1, rows, D_MODEL), lambda i, n: (i, 0, n)),
        out_shape=jax.ShapeDtypeStruct((depth, rows, w_ada.shape[2]), F32),
        compiler_params=_cparams("parallel", "parallel"),
        name="ada",
    )(cc, w_ada, b_ada.reshape(depth, 1, -1))


def _inproj_kernel(x_ref, mod_ref, g_ref, w_ref, o_ref):
    h = _norm_mod(x_ref[0], g_ref[...], mod_ref[0, 0:1, :], mod_ref[0, 1:2, :])
    o_ref[0] = _dot(h.astype(BF16), w_ref[...])


def _inproj(x, mod, g, w, tile):
    bn, ln, _ = x.shape
    n = w.shape[1]
    per_batch = mod.shape[0] > 1
    return pl.pallas_call(
        _inproj_kernel,
        grid=(bn, ln // tile),
        in_specs=[
            pl.BlockSpec((1, tile, D_MODEL), lambda b, t: (b, t, 0)),
            pl.BlockSpec((1, 6, D_MODEL), (lambda b, t: (b, 0, 0)) if per_batch else (lambda b, t: (0, 0, 0))),
            pl.BlockSpec((1, D_MODEL), lambda b, t: (0, 0)),
            pl.BlockSpec((D_MODEL, n), lambda b, t: (0, 0)),
        ],
        out_specs=pl.BlockSpec((1, tile, n), lambda b, t: (b, t, 0)),
        out_shape=jax.ShapeDtypeStruct((bn, ln, n), F32),
        compiler_params=_cparams("parallel", "parallel"),
        name="inproj",
    )(x, mod, g, w)


def _dft_kernel(a_ref, cs_ref, m_ref, o_ref, z_ref, *, seq, scale):
    @pl.when(pl.program_id(1) == 0)
    def _():
        for g in range(A_GROUPS):
            cols = slice(g * A_GROUP_DIM, (g + 1) * A_GROUP_DIM)
            xg = _dot(a_ref[0, :, cols].astype(BF16), cs_ref[...])
            z_ref[0:seq, cols] = xg[:, :A_GROUP_DIM].astype(BF16)
            z_ref[seq:2 * seq, cols] = xg[:, A_GROUP_DIM:].astype(BF16)

    o_ref[0] = (_dot(m_ref[...], z_ref[...]) * scale).astype(o_ref.dtype)


def _dft_matrices(seq):
    def cos_sin(n):
        idx = jnp.arange(n, dtype=jnp.int32)
        ang = ((idx[:, None] * idx[None, :]) % n).astype(F32) * (2.0 * jnp.pi / n)
        return jnp.cos(ang), jnp.sin(ang)
    cc, sc = cos_sin(A_GROUP_DIM)
    cl, sl = cos_sin(seq)
    return (jnp.concatenate([cc, sc], axis=1).astype(BF16),
            jnp.concatenate([cl, -sl], axis=1).astype(BF16))


def _dft(p, seq_block):
    bn, ln, _ = p.shape
    cs, m = _dft_matrices(ln)
    scale = float(1.0 / (ln * A_GROUP_DIM) ** 0.5)
    return pl.pallas_call(
        functools.partial(_dft_kernel, seq=ln, scale=scale),
        grid=(bn, ln // seq_block),
        in_specs=[
            pl.BlockSpec((1, ln, D_A), lambda b, k: (b, 0, 0)),
            pl.BlockSpec((A_GROUP_DIM, 2 * A_GROUP_DIM), lambda b, k: (0, 0)),
            pl.BlockSpec((seq_block, 2 * ln), lambda b, k: (k, 0)),
        ],
        out_specs=pl.BlockSpec((1, seq_block, D_A), lambda b, k: (b, k, 0)),
        out_shape=jax.ShapeDtypeStruct((bn, ln, D_A), BF16),
        scratch_shapes=[pltpu.VMEM((2 * ln, D_A), BF16)],
        compiler_params=_cparams("parallel", "arbitrary"),
        name="dft",
    )(p, cs, m)


def _convmod_kernel(*refs, seq):
    ncb = D_B // LANES
    u_refs, g_refs = refs[0:ncb], refs[ncb:2 * ncb]
    cw_ref, lg_ref, lb_ref, o_ref = refs[2 * ncb:2 * ncb + 4]
    e_refs, y_refs = refs[2 * ncb + 4:3 * ncb + 4], refs[3 * ncb + 4:4 * ncb + 4]
    grp = seq // SUBLANES
    halo = CONV_PAD * SUBLANES
    sub = lax.broadcasted_iota(jnp.int32, (halo, 1), 0) % SUBLANES

    rows = CONV_BLOCK * SUBLANES
    glu_rows = 128

    for c in range(ncb):
        cols = slice(c * LANES, (c + 1) * LANES)

        def glu(i, carry, c=c):
            r0 = pl.multiple_of(i * glu_rows, glu_rows)
            y_refs[c][pl.ds(r0, glu_rows), :] = (
                u_refs[c][0, pl.ds(r0, glu_rows), :] * _sigmoid(g_refs[c][0, pl.ds(r0, glu_rows), :]))
            return carry

        lax.fori_loop(0, seq // glu_rows, glu, 0)

        def permute(i, carry, c=c):
            for j in range(SUBLANES):
                k = i * SUBLANES + j
                dst = pl.multiple_of(halo + k * SUBLANES, SUBLANES)
                e_refs[c][pl.ds(dst, SUBLANES), :] = y_refs[c][pl.ds(k, SUBLANES, stride=grp), :]
            return carry

        lax.fori_loop(0, grp // SUBLANES, permute, 0)
        first = e_refs[c][halo:2 * halo, :]
        lastb = e_refs[c][seq:seq + halo, :]
        e_refs[c][0:halo, :] = jnp.where(sub == 0, 0.0, pltpu.roll(lastb, 1, 0))
        e_refs[c][seq + halo:seq + 2 * halo, :] = jnp.where(sub == SUBLANES - 1, 0.0, pltpu.roll(first, halo - 1, 0))

        taps = [cw_ref[t * SUBLANES:(t + 1) * SUBLANES, cols] for t in range(CONV_WIDTH)]

        def conv(i, carry, c=c, taps=taps):
            r0 = pl.multiple_of(i * rows, rows)
            acc = [[None, None] for _ in range(CONV_BLOCK)]
            for idx in range(CONV_BLOCK + CONV_WIDTH - 1):
                src = r0 + (idx + CONV_PAD - CONV_WIDTH // 2) * SUBLANES
                xin = e_refs[c][pl.ds(src, SUBLANES), :]
                for a in range(CONV_BLOCK):
                    t = idx - a
                    if 0 <= t < CONV_WIDTH:
                        term = xin * taps[t]
                        acc[a][t % 2] = term if acc[a][t % 2] is None else acc[a][t % 2] + term
            for a in range(CONV_BLOCK):
                y_refs[c][pl.ds(r0 + a * SUBLANES, SUBLANES), :] = acc[a][0] + acc[a][1]
            return carry

        lax.fori_loop(0, grp // CONV_BLOCK, conv, 0)

    def norm(i, carry):
        r0 = pl.multiple_of(i * glu_rows, glu_rows)
        accs = [y_refs[c][pl.ds(r0, glu_rows), :] for c in range(ncb)]
        mu = jnp.sum(sum(accs), axis=-1, keepdims=True) * (1.0 / D_B)
        cens = [a - mu for a in accs]
        var = jnp.sum(sum(a * a for a in cens), axis=-1, keepdims=True) * (1.0 / D_B)
        inv = lax.rsqrt(var + EPS)
        for c in range(ncb):
            cols = slice(c * LANES, (c + 1) * LANES)
            y_refs[c][pl.ds(r0, glu_rows), :] = _silu(cens[c] * inv * lg_ref[:, cols] + lb_ref[:, cols])
        return carry

    lax.fori_loop(0, seq // glu_rows, norm, 0, unroll=2)

    gb = grp // SUBLANES
    blk = SUBLANES * SUBLANES

    def unpermute(r2, carry):
        src = pl.multiple_of(r2 * 2 * blk, 2 * blk)
        dst = pl.multiple_of(r2 * 2 * SUBLANES, 2 * SUBLANES)
        for q in range(SUBLANES):
            for c in range(ncb):
                two = [y_refs[c][pl.ds(src + h * blk + q, SUBLANES, stride=SUBLANES), :] for h in range(2)]
                o_ref[0, pl.ds(dst + q * grp, 2 * SUBLANES), c * LANES:(c + 1) * LANES] = (
                    jnp.concatenate(two, axis=0).astype(o_ref.dtype))
        return carry

    lax.fori_loop(0, gb // 2, unpermute, 0)


def _convmod(p, conv_w, ln_g, ln_b):
    bn, ln, _ = p.shape
    ncb = D_B // LANES
    cw = jnp.repeat(conv_w, SUBLANES, axis=0)
    ucol, gcol = D_A // LANES, (D_A + D_B) // LANES
    col_spec = lambda cb: pl.BlockSpec((1, ln, LANES), lambda b: (b, 0, cb))
    return pl.pallas_call(
        functools.partial(_convmod_kernel, seq=ln),
        grid=(bn,),
        in_specs=[col_spec(ucol + c) for c in range(ncb)] + [col_spec(gcol + c) for c in range(ncb)] + [
            pl.BlockSpec((CONV_WIDTH * SUBLANES, D_B), lambda b: (0, 0)),
            pl.BlockSpec((1, D_B), lambda b: (0, 0)),
            pl.BlockSpec((1, D_B), lambda b: (0, 0)),
        ],
        out_specs=pl.BlockSpec((1, ln, D_B), lambda b: (b, 0, 0)),
        out_shape=jax.ShapeDtypeStruct((bn, ln, D_B), BF16),
        scratch_shapes=[pltpu.VMEM((ln + 2 * CONV_PAD * SUBLANES, LANES), F32) for _ in range(ncb)]
                       + [pltpu.VMEM((ln, LANES), F32) for _ in range(ncb)],
        compiler_params=_cparams("parallel"),
        name="convmod",
    )(*([p] * (2 * ncb)), cw, ln_g.reshape(1, -1), ln_b.reshape(1, -1))


def _outproj_kernel(x_ref, y1_ref, y2_ref, mod_ref, w1_ref, w2_ref, o_ref):
    y = _dot(y1_ref[0], w1_ref[...]) + _dot(y2_ref[0], w2_ref[...])
    o_ref[0] = x_ref[0] + mod_ref[0, 2:3, :] * y


def _outproj(x, y1, y2, mod, w1, w2, tile):
    bn, ln, _ = x.shape
    d1, d2 = y1.shape[2], y2.shape[2]
    per_batch = mod.shape[0] > 1
    return pl.pallas_call(
        _outproj_kernel,
        grid=(bn, ln // tile),
        in_specs=[
            pl.BlockSpec((1, tile, D_MODEL), lambda b, t: (b, t, 0)),
            pl.BlockSpec((1, tile, d1), lambda b, t: (b, t, 0)),
            pl.BlockSpec((1, tile, d2), lambda b, t: (b, t, 0)),
            pl.BlockSpec((1, 6, D_MODEL), (lambda b, t: (b, 0, 0)) if per_batch else (lambda b, t: (0, 0, 0))),
            pl.BlockSpec((d1, D_MODEL), lambda b, t: (0, 0)),
            pl.BlockSpec((d2, D_MODEL), lambda b, t: (0, 0)),
        ],
        out_specs=pl.BlockSpec((1, tile, D_MODEL), lambda b, t: (b, t, 0)),
        out_shape=jax.ShapeDtypeStruct((bn, ln, D_MODEL), F32),
        compiler_params=_cparams("parallel", "parallel"),
        name="outproj",
    )(x, y1, y2, mod, w1, w2)


def _ffn_kernel(xp_ref, x_ref, xn_ref, mod_ref, g_ref, wg_ref, wv_ref, cg_ref, cv_ref, wd_ref,
                o_ref, h_ref, acc_ref, *, tile):
    t = pl.program_id(1)
    last = pl.num_programs(1) - 1
    g = g_ref[...]
    shift, scale, gate = mod_ref[0, 3:4, :], mod_ref[0, 4:5, :], mod_ref[0, 5:6, :]
    ext = tile + 2 * FFN_HALO

    hp = _norm_mod(xp_ref[0], g, shift, scale)
    hn = _norm_mod(xn_ref[0], g, shift, scale)
    h_ref[0:FFN_HALO, :] = jnp.where(t > 0, hp, 0.0).astype(BF16)
    h_ref[FFN_HALO:FFN_HALO + tile, :] = _norm_mod(x_ref[0], g, shift, scale).astype(BF16)
    h_ref[FFN_HALO + tile:ext, :] = jnp.where(t < last, hn, 0.0).astype(BF16)
    acc_ref[...] = jnp.zeros_like(acc_ref)

    def conv3(up, cw):
        prev = pltpu.roll(up, 1, 0)
        nxt = pltpu.roll(up, ext - 1, 0)
        out = prev * cw[0:1, :] + up * cw[1:2, :] + nxt * cw[2:3, :]
        return out[FFN_HALO:FFN_HALO + tile, :]

    def up(j):
        hx = h_ref[...]
        return _dot(hx, wg_ref[j]), _dot(hx, wv_ref[j])

    nch = D_FF // FFN_CHUNK
    nxt = up(0)
    for j in range(nch):
        cur = nxt
        if j + 1 < nch:
            nxt = up(j + 1)
        act = _silu(conv3(cur[0], cg_ref[j])) * conv3(cur[1], cv_ref[j])
        acc_ref[...] += _dot(act.astype(BF16), wd_ref[j])
    o_ref[0] = x_ref[0] + gate * acc_ref[...]


def _ffn(x, mod, g, w_up, w_conv, w_down, tile):
    bn, ln, _ = x.shape
    nch = D_FF // FFN_CHUNK
    per_batch = mod.shape[0] > 1
    hb = tile // FFN_HALO
    nhb = ln // FFN_HALO
    wg = w_up[:, :D_FF].reshape(D_MODEL, nch, FFN_CHUNK).transpose(1, 0, 2).astype(BF16)
    wv = w_up[:, D_FF:].reshape(D_MODEL, nch, FFN_CHUNK).transpose(1, 0, 2).astype(BF16)
    wc = jnp.pad(w_conv, ((0, 5), (0, 0)))
    cg = wc[:, :D_FF].reshape(8, nch, FFN_CHUNK).transpose(1, 0, 2)
    cv = wc[:, D_FF:].reshape(8, nch, FFN_CHUNK).transpose(1, 0, 2)
    wd = w_down.reshape(nch, FFN_CHUNK, D_MODEL).astype(BF16)
    const3 = lambda b, t: (0, 0, 0)
    return pl.pallas_call(
        functools.partial(_ffn_kernel, tile=tile),
        grid=(bn, ln // tile),
        in_specs=[
            pl.BlockSpec((1, FFN_HALO, D_MODEL), lambda b, t: (b, jnp.maximum(t * hb - 1, 0), 0)),
            pl.BlockSpec((1, tile, D_MODEL), lambda b, t: (b, t, 0)),
            pl.BlockSpec((1, FFN_HALO, D_MODEL), lambda b, t: (b, jnp.minimum((t + 1) * hb, nhb - 1), 0)),
            pl.BlockSpec((1, 6, D_MODEL), (lambda b, t: (b, 0, 0)) if per_batch else const3),
            pl.BlockSpec((1, D_MODEL), lambda b, t: (0, 0)),
            pl.BlockSpec((nch, D_MODEL, FFN_CHUNK), const3),
            pl.BlockSpec((nch, D_MODEL, FFN_CHUNK), const3),
            pl.BlockSpec((nch, 8, FFN_CHUNK), const3),
            pl.BlockSpec((nch, 8, FFN_CHUNK), const3),
            pl.BlockSpec((nch, FFN_CHUNK, D_MODEL), const3),
        ],
        out_specs=pl.BlockSpec((1, tile, D_MODEL), lambda b, t: (b, t, 0)),
        out_shape=jax.ShapeDtypeStruct((bn, ln, D_MODEL), F32),
        scratch_shapes=[pltpu.VMEM((tile + 2 * FFN_HALO, D_MODEL), BF16),
                        pltpu.VMEM((tile, D_MODEL), F32)],
        compiler_params=_cparams("parallel", "parallel"),
        name="ffn",
    )(x, x, x, mod, g, wg, wv, cg, cv, wd)


def _head_norm(xb, ones_ref, gain):
    sq = xb * xb
    hi = sq.astype(BF16)
    lo = (sq - hi.astype(F32)).astype(BF16)
    ss = _dot(hi, ones_ref[...]) + _dot(lo, ones_ref[...])
    return xb * lax.rsqrt(ss * (1.0 / HEAD_DIM) + EPS) * gain


def _rope(y, cos, sin, first):
    partner = jnp.where(first, pltpu.roll(y, LANES - ROPE_PAIRS, 1), pltpu.roll(y, ROPE_PAIRS, 1))
    return y * cos + partner * sin


def _qkprep_kernel(p_ref, cos_ref, sin_ref, gq_ref, gk_ref, ones_ref, q_ref, kt_ref, v_ref):
    lane = lax.broadcasted_iota(jnp.int32, (1, LANES), 1)
    first = (lane % (2 * ROPE_PAIRS)) < ROPE_PAIRS
    cos, sin = cos_ref[...], sin_ref[...]
    for c in range(D_Q // LANES):
        cols = slice(c * LANES, (c + 1) * LANES)
        y = _head_norm(p_ref[0, :, cols], ones_ref, gq_ref[...])
        q_ref[0, :, cols] = (_rope(y, cos, sin, first) * Q_SCALE).astype(BF16)
    for c in range(D_KV // LANES):
        src = slice(D_Q + c * LANES, D_Q + (c + 1) * LANES)
        y = _head_norm(p_ref[0, :, src], ones_ref, gk_ref[...])
        kt_ref[0, c * LANES:(c + 1) * LANES, :] = _rope(y, cos, sin, first).T.astype(BF16)
    v_ref[0] = p_ref[0, :, D_Q + D_KV:D_Q + 2 * D_KV].astype(BF16)


def _kvprep_ctx_kernel(p_ref, gk_ref, ones_ref, kt_ref, v_ref):
    for c in range(D_KV // LANES):
        cols = slice(c * LANES, (c + 1) * LANES)
        kt_ref[0, cols, :] = _head_norm(p_ref[0, :, cols], ones_ref, gk_ref[...]).T.astype(BF16)
    v_ref[0] = p_ref[0, :, D_KV:2 * D_KV].astype(BF16)


def _group_ones():
    r = jnp.arange(LANES) // HEAD_DIM
    return (r[:, None] == r[None, :]).astype(BF16)


def _rope_tables(seq):
    t = jnp.arange(seq)
    freqs = ROPE_THETA ** (-jnp.arange(ROPE_PAIRS, dtype=F32) / ROPE_PAIRS)
    ang_r = (t // GRID_W).astype(F32)[:, None] * freqs
    ang_c = (t % GRID_W).astype(F32)[:, None] * freqs
    cos = jnp.concatenate([jnp.cos(ang_r)] * 2 + [jnp.cos(ang_c)] * 2, axis=1)
    sin = jnp.concatenate([-jnp.sin(ang_r), jnp.sin(ang_r), -jnp.sin(ang_c), jnp.sin(ang_c)], axis=1)
    return jnp.tile(cos, (1, LANES // HEAD_DIM)), jnp.tile(sin, (1, LANES // HEAD_DIM))


def _qkprep(p, q_g, k_g, tile):
    bn, ln, n = p.shape
    cos, sin = _rope_tables(ln)
    gq = jnp.tile(q_g, LANES // HEAD_DIM).reshape(1, LANES)
    gk = jnp.tile(k_g, LANES // HEAD_DIM).reshape(1, LANES)
    row = lambda b, t: (b, t, 0)
    tab = lambda b, t: (t, 0)
    const = lambda b, t: (0, 0)
    return pl.pallas_call(
        _qkprep_kernel,
        grid=(bn, ln // tile),
        in_specs=[
            pl.BlockSpec((1, tile, n), row),
            pl.BlockSpec((tile, LANES), tab),
            pl.BlockSpec((tile, LANES), tab),
            pl.BlockSpec((1, LANES), const),
            pl.BlockSpec((1, LANES), const),
            pl.BlockSpec((LANES, LANES), const),
        ],
        out_specs=[pl.BlockSpec((1, tile, D_Q), row),
                   pl.BlockSpec((1, D_KV, tile), lambda b, t: (b, 0, t)),
                   pl.BlockSpec((1, tile, D_KV), row)],
        out_shape=[jax.ShapeDtypeStruct((bn, ln, D_Q), BF16),
                   jax.ShapeDtypeStruct((bn, D_KV, ln), BF16),
                   jax.ShapeDtypeStruct((bn, ln, D_KV), BF16)],
        compiler_params=_cparams("parallel", "parallel"),
        name="qkprep",
    )(p, cos, sin, gq, gk, _group_ones())


def _kvprep_ctx(pc, k_g):
    bn, lc, n = pc.shape
    gk = jnp.tile(k_g, LANES // HEAD_DIM).reshape(1, LANES)
    return pl.pallas_call(
        _kvprep_ctx_kernel,
        grid=(bn,),
        in_specs=[
            pl.BlockSpec((1, lc, n), lambda b: (b, 0, 0)),
            pl.BlockSpec((1, LANES), lambda b: (0, 0)),
            pl.BlockSpec((LANES, LANES), lambda b: (0, 0)),
        ],
        out_specs=[pl.BlockSpec((1, D_KV, lc), lambda b: (b, 0, 0)),
                   pl.BlockSpec((1, lc, D_KV), lambda b: (b, 0, 0))],
        out_shape=[jax.ShapeDtypeStruct((bn, D_KV, lc), BF16),
                   jax.ShapeDtypeStruct((bn, lc, D_KV), BF16)],
        compiler_params=_cparams("parallel"),
        name="kvprep_ctx",
    )(pc, gk, _group_ones())


def _attn_kernel(q_ref, kt_ref, v_ref, o_ref, vlo_ref, vhi_ref):
    low = lax.broadcasted_iota(jnp.int32, (1, LANES), 1) < HEAD_DIM

    @pl.when(pl.program_id(2) == 0)
    def _():
        v = v_ref[0]
        one = jnp.ones_like(v)
        vlo_ref[...] = jnp.where(low, v, one)
        vhi_ref[...] = jnp.where(low, one, v)

    kt = kt_ref[0]
    for r0 in range(0, q_ref.shape[1], ATTN_SUB):
        rows = slice(r0, r0 + ATTN_SUB)
        for c in range(q_ref.shape[2] // LANES):
            cols = slice(c * LANES, (c + 1) * LANES)
            qp = q_ref[0, rows, cols]
            zero = jnp.zeros_like(qp)
            halves = []
            for qm, vm_ref in ((jnp.where(low, qp, zero), vlo_ref), (jnp.where(low, zero, qp), vhi_ref)):
                s = _dot(qm, kt)
                e = jnp.exp2(s - jnp.max(s, axis=-1, keepdims=True))
                o = _dot(e.astype(BF16), vm_ref[...])
                halves.append(o / pltpu.roll(o, HEAD_DIM, 1))
            o_ref[0, rows, cols] = jnp.where(low, halves[0], halves[1]).astype(o_ref.dtype)


def _attention(q, kt_all, v_all, tile):
    bn, ln, _ = q.shape
    lk = v_all.shape[1]
    qw = D_Q // 2
    kw = D_KV // 2
    return pl.pallas_call(
        _attn_kernel,
        grid=(bn, 2, ln // tile),
        in_specs=[
            pl.BlockSpec((1, tile, qw), lambda b, j, t: (b, t, j)),
            pl.BlockSpec((1, kw, lk), lambda b, j, t: (b, j, 0)),
            pl.BlockSpec((1, lk, kw), lambda b, j, t: (b, 0, j)),
        ],
        out_specs=pl.BlockSpec((1, tile, qw), lambda b, j, t: (b, t, j)),
        out_shape=jax.ShapeDtypeStruct((bn, ln, D_Q), BF16),
        scratch_shapes=[pltpu.VMEM((lk, kw), BF16), pltpu.VMEM((lk, kw), BF16)],
        compiler_params=_cparams("parallel", "parallel", "arbitrary"),
        name="attention",
    )(q, kt_all, v_all)


def _pool_kernel(u_ref, cnt_ref, w_ref, sc_ref, o_ref, pad_ref, *, seq):
    n = seq + 2 * POOL_PAD
    zeros = jnp.zeros((POOL_PAD, D_POOL), F32)
    pad_ref[0:POOL_PAD, :] = zeros
    pad_ref[seq + POOL_PAD:n, :] = zeros
    pad_ref[POOL_PAD:seq + POOL_PAD, :] = u_ref[0]
    a = pad_ref[...]
    w2 = a + pltpu.roll(a, 1, 0)
    w4 = pltpu.roll(w2, 1, 0) + pltpu.roll(w2, n - 1, 0)
    w8 = pltpu.roll(w4, 2, 0) + pltpu.roll(w4, n - 2, 0)
    w16 = pltpu.roll(w8, 4, 0) + pltpu.roll(w8, n - 4, 0)
    lane = lax.broadcasted_iota(jnp.int32, (1, D_POOL), 1)
    g = lane // POOL_GROUP_DIM
    win = jnp.where(g == 0, w2, jnp.where(g == 1, w4, jnp.where(g == 2, w8, w16)))
    u = u_ref[0]
    pooled = win[POOL_PAD:seq + POOL_PAD, :] / cnt_ref[...] - u
    o_ref[0] = (_dot(pooled.astype(BF16), w_ref[...]) * sc_ref[...]).astype(o_ref.dtype)


def _pool(p, pool_w, pool_scale):
    bn, ln, n = p.shape
    t = jnp.arange(ln)
    cnt = jnp.concatenate(
        [jnp.broadcast_to((jnp.minimum(t + w // 2, ln) - jnp.maximum(t - w // 2, 0)).astype(F32)[:, None],
                          (ln, POOL_GROUP_DIM)) for w in POOL_WINDOWS], axis=1)
    wbd = jax.scipy.linalg.block_diag(*[pool_w[i] for i in range(pool_w.shape[0])]).astype(BF16)
    return pl.pallas_call(
        functools.partial(_pool_kernel, seq=ln),
        grid=(bn,),
        in_specs=[
            pl.BlockSpec((1, ln, D_POOL), lambda b: (b, 0, n // D_POOL - 1)),
            pl.BlockSpec((ln, D_POOL), lambda b: (0, 0)),
            pl.BlockSpec((D_POOL, D_POOL), lambda b: (0, 0)),
            pl.BlockSpec((1, D_POOL), lambda b: (0, 0)),
        ],
        out_specs=pl.BlockSpec((1, ln, D_POOL), lambda b: (b, 0, 0)),
        out_shape=jax.ShapeDtypeStruct((bn, ln, D_POOL), BF16),
        scratch_shapes=[pltpu.VMEM((ln + 2 * POOL_PAD, D_POOL), F32)],
        compiler_params=_cparams("parallel"),
        name="pool",
    )(p, cnt, wbd, pool_scale.reshape(1, -1))


def _paired_head_order():
    order = []
    per_kv = N_Q_HEADS // N_KV_HEADS
    for j in range(N_KV_HEADS // 2):
        for i in range(per_kv):
            order += [(2 * j) * per_kv + i, (2 * j + 1) * per_kv + i]
    return order


def _even_layer(x, mod, norm1_g, norm2_g, w_in, conv_w, ln_g, ln_b, w_out, w_up, w_conv, w_down, tile):
    p = _inproj(x, mod, norm1_g.reshape(1, -1), w_in.astype(BF16), tile)
    fa = _dft(p, min(512, p.shape[1]))
    bb = _convmod(p, conv_w, ln_g, ln_b)
    wo = w_out.astype(BF16)
    x = _outproj(x, fa, bb, mod, wo[:D_A], wo[D_A:], tile)
    return _ffn(x, mod, norm2_g.reshape(1, -1), w_up, w_conv, w_down, tile)


def _odd_layer_last(x, ctx, mod, modc, norm1_g, norm2_g, w_in, q_g, k_g, pool_w, pool_scale, w_out,
                    w_up, w_conv, w_down, tile):
    heads = jnp.asarray(_paired_head_order())
    qcols = (heads[:, None] * HEAD_DIM + jnp.arange(HEAD_DIM)[None, :]).reshape(-1)
    w_in_b = w_in.astype(BF16)
    w_lat = jnp.concatenate([w_in_b[:, qcols], w_in_b[:, D_Q:]], axis=1)
    g1 = norm1_g.reshape(1, -1)
    p = _inproj(x, mod, g1, w_lat, tile)
    q, kt, v = _qkprep(p, q_g, k_g, tile)
    pc = _inproj(ctx, modc, g1, w_in_b[:, D_Q:D_Q + 2 * D_KV], ctx.shape[1])
    kct, vc = _kvprep_ctx(pc, k_g)
    attn = _attention(q, jnp.concatenate([kct, kt], axis=2), jnp.concatenate([vc, v], axis=1), ATTN_TILE)
    pooled = _pool(p, pool_w, pool_scale)
    wo = w_out.astype(BF16)
    x = _outproj(x, attn, pooled, mod, wo[:D_Q][qcols], wo[D_Q:], tile)
    return _ffn(x, mod, norm2_g.reshape(1, -1), w_up, w_conv, w_down, tile)


def kernel(x, c, ctx, c_ctx, w_ada, b_ada, norm1_g, norm2_g, ev_w_in, ev_conv_w, ev_ln_g, ev_ln_b, ev_w_out,
           od_w_in, od_q_g, od_k_g, od_pool_w, od_pool_scale, od_w_out, ffn_w_up, ffn_conv_w, ffn_w_down):
    depth = w_ada.shape[0]
    assert depth == 2, "even layer followed by a final odd layer"
    bn = x.shape[0]
    rows = -(-(bn + 1) // 8) * 8
    cc = jnp.concatenate([c, c_ctx[None, :], jnp.zeros((rows - bn - 1, D_MODEL), F32)], axis=0)
    mods = _ada(cc, w_ada, b_ada)
    mod = [mods[i, :bn].reshape(bn, 6, D_MODEL) for i in range(depth)]
    modc = [mods[i, bn:bn + 1].reshape(1, 6, D_MODEL) for i in range(depth)]

    tile = 512
    ev = (ev_w_in[0], ev_conv_w[0], ev_ln_g[0], ev_ln_b[0], ev_w_out[0],
          ffn_w_up[0], ffn_conv_w[0], ffn_w_down[0])
    x = _even_layer(x, mod[0], norm1_g[0], norm2_g[0], *ev, tile)
    ctx = _even_layer(ctx, modc[0], norm1_g[0], norm2_g[0], *ev, ctx.shape[1])
    return _odd_layer_last(x, ctx, mod[1], modc[1], norm1_g[1], norm2_g[1], od_w_in[0], od_q_g[0], od_k_g[0],
                           od_pool_w[0], od_pool_scale[0], od_w_out[0],
                           ffn_w_up[1], ffn_conv_w[1], ffn_w_down[1], tile)
```

```python
import functools

import jax
import jax.numpy as jnp
import numpy as np
from jax import lax
from jax.experimental import pallas as pl
from jax.experimental.pallas import tpu as pltpu

F32 = jnp.float32
BF16 = jnp.bfloat16

D_MODEL = 1024
GRID_W = 64
EPS = 1e-6

A_GROUPS = 4
A_GROUP_DIM = 128
D_A = A_GROUPS * A_GROUP_DIM
D_B = 512
CONV_WIDTH = 31
CONV_PAD = 16
CONV_BLOCK = 8

HEAD_DIM = 64
N_Q_HEADS = 12
N_KV_HEADS = 4
D_Q = N_Q_HEADS * HEAD_DIM
D_KV = N_KV_HEADS * HEAD_DIM
POOL_WINDOWS = (2, 4, 8, 16)
POOL_GROUP_DIM = 64
D_POOL = len(POOL_WINDOWS) * POOL_GROUP_DIM
POOL_PAD = 16
ROPE_THETA = 10000.0
ROPE_PAIRS = HEAD_DIM // 4
Q_SCALE = HEAD_DIM ** -0.5 * 1.4426950408889634

D_FF = 2816
FFN_CHUNK = 256
ATTN_TILE = 1024
ATTN_SUB = 512
FFN_HALO = 16

LANES = 128
SUBLANES = 8
VMEM_LIMIT = 56 * 1024 * 1024


def _cparams(*sem):
    return pltpu.CompilerParams(dimension_semantics=sem, vmem_limit_bytes=VMEM_LIMIT)


def _sigmoid(x):
    return 1.0 / (1.0 + jnp.exp(-x))


def _silu(x):
    return x * _sigmoid(x)


def _norm_mod(x, g, shift, scale):
    ms = jnp.mean(x * x, axis=-1, keepdims=True)
    y = x * lax.rsqrt(ms + EPS) * g
    return y * (1.0 + scale) + shift


def _dot(a, b):
    return jnp.dot(a, b, preferred_element_type=F32)


def _ada_kernel(cc_ref, w_ref, b_ref, o_ref):
    s = _silu(cc_ref[...])
    o_ref[0] = _dot(s.astype(BF16), w_ref[0].astype(BF16)) + b_ref[0]


def _ada(cc, w_ada, b_ada):
    depth = w_ada.shape[0]
    rows = cc.shape[0]
    nblk = w_ada.shape[2] // D_MODEL
    return pl.pallas_call(
        _ada_kernel,
        grid=(depth, nblk),
        in_specs=[
            pl.BlockSpec((rows, D_MODEL), lambda i, n: (0, 0)),
            pl.BlockSpec((1, D_MODEL, D_MODEL), lambda i, n: (i, 0, n)),
            pl.BlockSpec((1, 1, D_MODEL), lambda i, n: (i, 0, n)),
        ],
        out_specs=pl.BlockSpec((1, rows, D_MODEL), lambda i, n: (i, 0, n)),
        out_shape=jax.ShapeDtypeStruct((depth, rows, w_ada.shape[2]), F32),
        compiler_params=_cparams("parallel", "parallel"),
        name="ada",
    )(cc, w_ada, b_ada.reshape(depth, 1, -1))


def _inproj_kernel(x_ref, mod_ref, g_ref, w_ref, o_ref):
    h = _norm_mod(x_ref[0], g_ref[...], mod_ref[0, 0:1, :], mod_ref[0, 1:2, :])
    o_ref[0] = _dot(h.astype(BF16), w_ref[...])


def _inproj(x, mod, g, w, tile):
    bn, ln, _ = x.shape
    n = w.shape[1]
    per_batch = mod.shape[0] > 1
    return pl.pallas_call(
        _inproj_kernel,
        grid=(bn, ln // tile),
        in_specs=[
            pl.BlockSpec((1, tile, D_MODEL), lambda b, t: (b, t, 0)),
            pl.BlockSpec((1, 6, D_MODEL), (lambda b, t: (b, 0, 0)) if per_batch else (lambda b, t: (0, 0, 0))),
            pl.BlockSpec((1, D_MODEL), lambda b, t: (0, 0)),
            pl.BlockSpec((D_MODEL, n), lambda b, t: (0, 0)),
        ],
        out_specs=pl.BlockSpec((1, tile, n), lambda b, t: (b, t, 0)),
        out_shape=jax.ShapeDtypeStruct((bn, ln, n), F32),
        compiler_params=_cparams("parallel", "parallel"),
        name="inproj",
    )(x, mod, g, w)


def _dft_kernel(a_ref, cs_ref, m_ref, o_ref, z_ref, *, seq, scale):
    @pl.when(pl.program_id(1) == 0)
    def _():
        for g in range(A_GROUPS):
            cols = slice(g * A_GROUP_DIM, (g + 1) * A_GROUP_DIM)
            xg = _dot(a_ref[0, :, cols].astype(BF16), cs_ref[...])
            z_ref[0:seq, cols] = xg[:, :A_GROUP_DIM].astype(BF16)
            z_ref[seq:2 * seq, cols] = xg[:, A_GROUP_DIM:].astype(BF16)

    o_ref[0] = (_dot(m_ref[...], z_ref[...]) * scale).astype(o_ref.dtype)


def _dft_matrices(seq):
    def cos_sin(n):
        idx = np.arange(n, dtype=np.int64)
        ang = ((idx[:, None] * idx[None, :]) % n).astype(np.float64) * (2.0 * np.pi / n)
        return np.cos(ang), np.sin(ang)
    cc, sc = cos_sin(A_GROUP_DIM)
    cl, sl = cos_sin(seq)
    return (jnp.asarray(np.concatenate([cc, sc], axis=1).astype(BF16)),
            jnp.asarray(np.concatenate([cl, -sl], axis=1).astype(BF16)))


def _dft(p, seq_block):
    bn, ln, _ = p.shape
    cs, m = _dft_matrices(ln)
    scale = float(1.0 / (ln * A_GROUP_DIM) ** 0.5)
    return pl.pallas_call(
        functools.partial(_dft_kernel, seq=ln, scale=scale),
        grid=(bn, ln // seq_block),
        in_specs=[
            pl.BlockSpec((1, ln, D_A), lambda b, k: (b, 0, 0)),
            pl.BlockSpec((A_GROUP_DIM, 2 * A_GROUP_DIM), lambda b, k: (0, 0)),
            pl.BlockSpec((seq_block, 2 * ln), lambda b, k: (k, 0)),
        ],
        out_specs=pl.BlockSpec((1, seq_block, D_A), lambda b, k: (b, k, 0)),
        out_shape=jax.ShapeDtypeStruct((bn, ln, D_A), BF16),
        scratch_shapes=[pltpu.VMEM((2 * ln, D_A), BF16)],
        compiler_params=_cparams("parallel", "arbitrary"),
        name="dft",
    )(p, cs, m)


def _convmod_kernel(*refs, seq):
    ncb = D_B // LANES
    u_refs, g_refs = refs[0:ncb], refs[ncb:2 * ncb]
    cw_ref, lg_ref, lb_ref, o_ref = refs[2 * ncb:2 * ncb + 4]
    e_refs, y_refs = refs[2 * ncb + 4:3 * ncb + 4], refs[3 * ncb + 4:4 * ncb + 4]
    grp = seq // SUBLANES
    halo = CONV_PAD * SUBLANES
    sub = lax.broadcasted_iota(jnp.int32, (halo, 1), 0) % SUBLANES

    rows = CONV_BLOCK * SUBLANES
    glu_rows = 128

    for c in range(ncb):
        cols = slice(c * LANES, (c + 1) * LANES)

        def glu(i, carry, c=c):
            r0 = pl.multiple_of(i * glu_rows, glu_rows)
            y_refs[c][pl.ds(r0, glu_rows), :] = (
                u_refs[c][0, pl.ds(r0, glu_rows), :] * _sigmoid(g_refs[c][0, pl.ds(r0, glu_rows), :]))
            return carry

        lax.fori_loop(0, seq // glu_rows, glu, 0)

        def permute(i, carry, c=c):
            for j in range(SUBLANES):
                k = i * SUBLANES + j
                dst = pl.multiple_of(halo + k * SUBLANES, SUBLANES)
                e_refs[c][pl.ds(dst, SUBLANES), :] = y_refs[c][pl.ds(k, SUBLANES, stride=grp), :]
            return carry

        lax.fori_loop(0, grp // SUBLANES, permute, 0)
        first = e_refs[c][halo:2 * halo, :]
        lastb = e_refs[c][seq:seq + halo, :]
        e_refs[c][0:halo, :] = jnp.where(sub == 0, 0.0, pltpu.roll(lastb, 1, 0))
        e_refs[c][seq + halo:seq + 2 * halo, :] = jnp.where(sub == SUBLANES - 1, 0.0, pltpu.roll(first, halo - 1, 0))

        taps = [cw_ref[t * SUBLANES:(t + 1) * SUBLANES, cols] for t in range(CONV_WIDTH)]

        def conv(i, carry, c=c, taps=taps):
            r0 = pl.multiple_of(i * rows, rows)
            acc = [[None, None] for _ in range(CONV_BLOCK)]
            for idx in range(CONV_BLOCK + CONV_WIDTH - 1):
                src = r0 + (idx + CONV_PAD - CONV_WIDTH // 2) * SUBLANES
                xin = e_refs[c][pl.ds(src, SUBLANES), :]
                for a in range(CONV_BLOCK):
                    t = idx - a
                    if 0 <= t < CONV_WIDTH:
                        term = xin * taps[t]
                        acc[a][t % 2] = term if acc[a][t % 2] is None else acc[a][t % 2] + term
            for a in range(CONV_BLOCK):
                y_refs[c][pl.ds(r0 + a * SUBLANES, SUBLANES), :] = acc[a][0] + acc[a][1]
            return carry

        lax.fori_loop(0, grp // CONV_BLOCK, conv, 0)

    def norm(i, carry):
        r0 = pl.multiple_of(i * glu_rows, glu_rows)
        accs = [y_refs[c][pl.ds(r0, glu_rows), :] for c in range(ncb)]
        mu = jnp.sum(sum(accs), axis=-1, keepdims=True) * (1.0 / D_B)
        cens = [a - mu for a in accs]
        var = jnp.sum(sum(a * a for a in cens), axis=-1, keepdims=True) * (1.0 / D_B)
        inv = lax.rsqrt(var + EPS)
        for c in range(ncb):
            cols = slice(c * LANES, (c + 1) * LANES)
            y_refs[c][pl.ds(r0, glu_rows), :] = _silu(cens[c] * inv * lg_ref[:, cols] + lb_ref[:, cols])
        return carry

    lax.fori_loop(0, seq // glu_rows, norm, 0, unroll=2)

    gb = grp // SUBLANES
    blk = SUBLANES * SUBLANES

    def unpermute(r2, carry):
        src = pl.multiple_of(r2 * 2 * blk, 2 * blk)
        dst = pl.multiple_of(r2 * 2 * SUBLANES, 2 * SUBLANES)
        for q in range(SUBLANES):
            for c in range(ncb):
                two = [y_refs[c][pl.ds(src + h * blk + q, SUBLANES, stride=SUBLANES), :] for h in range(2)]
                o_ref[0, pl.ds(dst + q * grp, 2 * SUBLANES), c * LANES:(c + 1) * LANES] = (
                    jnp.concatenate(two, axis=0).astype(o_ref.dtype))
        return carry

    lax.fori_loop(0, gb // 2, unpermute, 0)


def _convmod(p, conv_w, ln_g, ln_b):
    bn, ln, _ = p.shape
    ncb = D_B // LANES
    cw = jnp.repeat(conv_w, SUBLANES, axis=0)
    ucol, gcol = D_A // LANES, (D_A + D_B) // LANES
    col_spec = lambda cb: pl.BlockSpec((1, ln, LANES), lambda b: (b, 0, cb))
    return pl.pallas_call(
        functools.partial(_convmod_kernel, seq=ln),
        grid=(bn,),
        in_specs=[col_spec(ucol + c) for c in range(ncb)] + [col_spec(gcol + c) for c in range(ncb)] + [
            pl.BlockSpec((CONV_WIDTH * SUBLANES, D_B), lambda b: (0, 0)),
            pl.BlockSpec((1, D_B), lambda b: (0, 0)),
            pl.BlockSpec((1, D_B), lambda b: (0, 0)),
        ],
        out_specs=pl.BlockSpec((1, ln, D_B), lambda b: (b, 0, 0)),
        out_shape=jax.ShapeDtypeStruct((bn, ln, D_B), BF16),
        scratch_shapes=[pltpu.VMEM((ln + 2 * CONV_PAD * SUBLANES, LANES), F32) for _ in range(ncb)]
                       + [pltpu.VMEM((ln, LANES), F32) for _ in range(ncb)],
        compiler_params=_cparams("parallel"),
        name="convmod",
    )(*([p] * (2 * ncb)), cw, ln_g.reshape(1, -1), ln_b.reshape(1, -1))


def _block_kernel(xp_ref, x_ref, xn_ref, ap_ref, a_ref, an_ref, bp_ref, b_ref, bn_ref, mod_ref, g_ref,
                  w1_ref, w2_ref, wg_ref, wv_ref, cg_ref, cv_ref, wd_ref, o_ref, h_ref, x1_ref, acc_ref, *, tile):
    t = pl.program_id(1)
    last = pl.num_programs(1) - 1
    g = g_ref[...]
    gate1 = mod_ref[0, 2:3, :]
    shift, scale, gate2 = mod_ref[0, 3:4, :], mod_ref[0, 4:5, :], mod_ref[0, 5:6, :]
    ext = tile + 2 * FFN_HALO

    def mixed(xr, ar, br):
        return xr[0] + gate1 * (_dot(ar[0], w1_ref[...]) + _dot(br[0], w2_ref[...]))

    x1_ref[...] = mixed(x_ref, a_ref, b_ref)
    hp = _norm_mod(mixed(xp_ref, ap_ref, bp_ref), g, shift, scale)
    hn = _norm_mod(mixed(xn_ref, an_ref, bn_ref), g, shift, scale)
    h_ref[0:FFN_HALO, :] = jnp.where(t > 0, hp, 0.0).astype(BF16)
    h_ref[FFN_HALO:FFN_HALO + tile, :] = _norm_mod(x1_ref[...], g, shift, scale).astype(BF16)
    h_ref[FFN_HALO + tile:ext, :] = jnp.where(t < last, hn, 0.0).astype(BF16)
    acc_ref[...] = jnp.zeros_like(acc_ref)

    def conv3(up, cw):
        prev = pltpu.roll(up, 1, 0)
        nxt = pltpu.roll(up, ext - 1, 0)
        out = prev * cw[0:1, :] + up * cw[1:2, :] + nxt * cw[2:3, :]
        return out[FFN_HALO:FFN_HALO + tile, :]

    def up(j):
        hx = h_ref[...]
        return _dot(hx, wg_ref[j]), _dot(hx, wv_ref[j])

    nch = D_FF // FFN_CHUNK
    nxt = up(0)
    for j in range(nch):
        cur = nxt
        if j + 1 < nch:
            nxt = up(j + 1)
        act = _silu(conv3(cur[0], cg_ref[j])) * conv3(cur[1], cv_ref[j])
        acc_ref[...] += _dot(act.astype(BF16), wd_ref[j])
    o_ref[0] = x1_ref[...] + gate2 * acc_ref[...]


def _block(x, y1, y2, mod, g, w1, w2, w_up, w_conv, w_down, tile):
    bn, ln, _ = x.shape
    d1, d2 = y1.shape[2], y2.shape[2]
    nch = D_FF // FFN_CHUNK
    per_batch = mod.shape[0] > 1
    hb = tile // FFN_HALO
    nhb = ln // FFN_HALO
    wg = w_up[:, :D_FF].reshape(D_MODEL, nch, FFN_CHUNK).transpose(1, 0, 2).astype(BF16)
    wv = w_up[:, D_FF:].reshape(D_MODEL, nch, FFN_CHUNK).transpose(1, 0, 2).astype(BF16)
    wc = jnp.pad(w_conv, ((0, 5), (0, 0)))
    cg = wc[:, :D_FF].reshape(8, nch, FFN_CHUNK).transpose(1, 0, 2)
    cv = wc[:, D_FF:].reshape(8, nch, FFN_CHUNK).transpose(1, 0, 2)
    wd = w_down.reshape(nch, FFN_CHUNK, D_MODEL).astype(BF16)
    const2 = lambda b, t: (0, 0)
    const3 = lambda b, t: (0, 0, 0)
    prev = lambda b, t: (b, jnp.maximum(t * hb - 1, 0), 0)
    main = lambda b, t: (b, t, 0)
    nxt = lambda b, t: (b, jnp.minimum((t + 1) * hb, nhb - 1), 0)

    def rows3(d):
        return [pl.BlockSpec((1, FFN_HALO, d), prev), pl.BlockSpec((1, tile, d), main),
                pl.BlockSpec((1, FFN_HALO, d), nxt)]

    return pl.pallas_call(
        functools.partial(_block_kernel, tile=tile),
        grid=(bn, ln // tile),
        in_specs=rows3(D_MODEL) + rows3(d1) + rows3(d2) + [
            pl.BlockSpec((1, 6, D_MODEL), (lambda b, t: (b, 0, 0)) if per_batch else const3),
            pl.BlockSpec((1, D_MODEL), const2),
            pl.BlockSpec((d1, D_MODEL), const2),
            pl.BlockSpec((d2, D_MODEL), const2),
            pl.BlockSpec((nch, D_MODEL, FFN_CHUNK), const3),
            pl.BlockSpec((nch, D_MODEL, FFN_CHUNK), const3),
            pl.BlockSpec((nch, 8, FFN_CHUNK), const3),
            pl.BlockSpec((nch, 8, FFN_CHUNK), const3),
            pl.BlockSpec((nch, FFN_CHUNK, D_MODEL), const3),
        ],
        out_specs=pl.BlockSpec((1, tile, D_MODEL), main),
        out_shape=jax.ShapeDtypeStruct((bn, ln, D_MODEL), F32),
        scratch_shapes=[pltpu.VMEM((tile + 2 * FFN_HALO, D_MODEL), BF16),
                        pltpu.VMEM((tile, D_MODEL), F32),
                        pltpu.VMEM((tile, D_MODEL), F32)],
        compiler_params=_cparams("parallel", "parallel"),
        name="block",
    )(x, x, x, y1, y1, y1, y2, y2, y2, mod, g, w1, w2, wg, wv, cg, cv, wd)


def _head_norm(xb, ones_ref, gain):
    sq = xb * xb
    hi = sq.astype(BF16)
    lo = (sq - hi.astype(F32)).astype(BF16)
    ss = _dot(hi, ones_ref[...]) + _dot(lo, ones_ref[...])
    return xb * lax.rsqrt(ss * (1.0 / HEAD_DIM) + EPS) * gain


def _rope(y, cos, sin, first):
    partner = jnp.where(first, pltpu.roll(y, LANES - ROPE_PAIRS, 1), pltpu.roll(y, ROPE_PAIRS, 1))
    return y * cos + partner * sin


def _inproj_qk_kernel(x_ref, mod_ref, g_ref, w_ref, cos_ref, sin_ref, gq_ref, gk_ref, ones_ref,
                      q_ref, kt_ref, v_ref, u_ref):
    h = _norm_mod(x_ref[0], g_ref[...], mod_ref[0, 0:1, :], mod_ref[0, 1:2, :])
    p = _dot(h.astype(BF16), w_ref[...])
    lane = lax.broadcasted_iota(jnp.int32, (1, LANES), 1)
    first = (lane % (2 * ROPE_PAIRS)) < ROPE_PAIRS
    cos, sin = cos_ref[...], sin_ref[...]
    for c in range(D_Q // LANES):
        cols = slice(c * LANES, (c + 1) * LANES)
        y = _head_norm(p[:, cols], ones_ref, gq_ref[...])
        q_ref[0, :, cols] = (_rope(y, cos, sin, first) * Q_SCALE).astype(BF16)
    for c in range(D_KV // LANES):
        src = slice(D_Q + c * LANES, D_Q + (c + 1) * LANES)
        y = _head_norm(p[:, src], ones_ref, gk_ref[...])
        kt_ref[0, c * LANES:(c + 1) * LANES, :] = _rope(y, cos, sin, first).T.astype(BF16)
    v_ref[0] = p[:, D_Q + D_KV:D_Q + 2 * D_KV].astype(BF16)
    u_ref[0] = p[:, D_Q + 2 * D_KV:]


def _kvprep_ctx_kernel(p_ref, gk_ref, ones_ref, kt_ref, v_ref):
    for c in range(D_KV // LANES):
        cols = slice(c * LANES, (c + 1) * LANES)
        kt_ref[0, cols, :] = _head_norm(p_ref[0, :, cols], ones_ref, gk_ref[...]).T.astype(BF16)
    v_ref[0] = p_ref[0, :, D_KV:2 * D_KV].astype(BF16)


def _group_ones():
    r = jnp.arange(LANES) // HEAD_DIM
    return (r[:, None] == r[None, :]).astype(BF16)


def _rope_tables(seq):
    t = jnp.arange(seq)
    freqs = ROPE_THETA ** (-jnp.arange(ROPE_PAIRS, dtype=F32) / ROPE_PAIRS)
    ang_r = (t // GRID_W).astype(F32)[:, None] * freqs
    ang_c = (t % GRID_W).astype(F32)[:, None] * freqs
    cos = jnp.concatenate([jnp.cos(ang_r)] * 2 + [jnp.cos(ang_c)] * 2, axis=1)
    sin = jnp.concatenate([-jnp.sin(ang_r), jnp.sin(ang_r), -jnp.sin(ang_c), jnp.sin(ang_c)], axis=1)
    return jnp.tile(cos, (1, LANES // HEAD_DIM)), jnp.tile(sin, (1, LANES // HEAD_DIM))


def _inproj_qk(x, mod, g, w, q_g, k_g, tile):
    bn, ln, _ = x.shape
    n = w.shape[1]
    cos, sin = _rope_tables(ln)
    gq = jnp.tile(q_g, LANES // HEAD_DIM).reshape(1, LANES)
    gk = jnp.tile(k_g, LANES // HEAD_DIM).reshape(1, LANES)
    row = lambda b, t: (b, t, 0)
    tab = lambda b, t: (t, 0)
    const = lambda b, t: (0, 0)
    return pl.pallas_call(
        _inproj_qk_kernel,
        grid=(bn, ln // tile),
        in_specs=[
            pl.BlockSpec((1, tile, D_MODEL), row),
            pl.BlockSpec((1, 6, D_MODEL), lambda b, t: (b, 0, 0)),
            pl.BlockSpec((1, D_MODEL), const),
            pl.BlockSpec((D_MODEL, n), const),
            pl.BlockSpec((tile, LANES), tab),
            pl.BlockSpec((tile, LANES), tab),
            pl.BlockSpec((1, LANES), const),
            pl.BlockSpec((1, LANES), const),
            pl.BlockSpec((LANES, LANES), const),
        ],
        out_specs=[pl.BlockSpec((1, tile, D_Q), row),
                   pl.BlockSpec((1, D_KV, tile), lambda b, t: (b, 0, t)),
                   pl.BlockSpec((1, tile, D_KV), row),
                   pl.BlockSpec((1, tile, D_POOL), row)],
        out_shape=[jax.ShapeDtypeStruct((bn, ln, D_Q), BF16),
                   jax.ShapeDtypeStruct((bn, D_KV, ln), BF16),
                   jax.ShapeDtypeStruct((bn, ln, D_KV), BF16),
                   jax.ShapeDtypeStruct((bn, ln, D_POOL), F32)],
        compiler_params=_cparams("parallel", "parallel"),
        name="inproj_qk",
    )(x, mod, g, w, cos, sin, gq, gk, _group_ones())


def _kvprep_ctx(pc, k_g):
    bn, lc, n = pc.shape
    gk = jnp.tile(k_g, LANES // HEAD_DIM).reshape(1, LANES)
    return pl.pallas_call(
        _kvprep_ctx_kernel,
        grid=(bn,),
        in_specs=[
            pl.BlockSpec((1, lc, n), lambda b: (b, 0, 0)),
            pl.BlockSpec((1, LANES), lambda b: (0, 0)),
            pl.BlockSpec((LANES, LANES), lambda b: (0, 0)),
        ],
        out_specs=[pl.BlockSpec((1, D_KV, lc), lambda b: (b, 0, 0)),
                   pl.BlockSpec((1, lc, D_KV), lambda b: (b, 0, 0))],
        out_shape=[jax.ShapeDtypeStruct((bn, D_KV, lc), BF16),
                   jax.ShapeDtypeStruct((bn, lc, D_KV), BF16)],
        compiler_params=_cparams("parallel"),
        name="kvprep_ctx",
    )(pc, gk, _group_ones())


def _attn_kernel(q_ref, kt_ref, v_ref, o_ref, vlo_ref, vhi_ref):
    low = lax.broadcasted_iota(jnp.int32, (1, LANES), 1) < HEAD_DIM

    @pl.when(pl.program_id(2) == 0)
    def _():
        v = v_ref[0]
        one = jnp.ones_like(v)
        vlo_ref[...] = jnp.where(low, v, one)
        vhi_ref[...] = jnp.where(low, one, v)

    kt = kt_ref[0]
    for r0 in range(0, q_ref.shape[1], ATTN_SUB):
        rows = slice(r0, r0 + ATTN_SUB)
        for c in range(q_ref.shape[2] // LANES):
            cols = slice(c * LANES, (c + 1) * LANES)
            qp = q_ref[0, rows, cols]
            zero = jnp.zeros_like(qp)
            halves = []
            for qm, vm_ref in ((jnp.where(low, qp, zero), vlo_ref), (jnp.where(low, zero, qp), vhi_ref)):
                s = _dot(qm, kt)
                e = jnp.exp2(s - jnp.max(s, axis=-1, keepdims=True))
                o = _dot(e.astype(BF16), vm_ref[...])
                halves.append(o / pltpu.roll(o, HEAD_DIM, 1))
            o_ref[0, rows, cols] = jnp.where(low, halves[0], halves[1]).astype(o_ref.dtype)


def _attention(q, kt_all, v_all, tile):
    bn, ln, _ = q.shape
    lk = v_all.shape[1]
    qw = D_Q // 2
    kw = D_KV // 2
    return pl.pallas_call(
        _attn_kernel,
        grid=(bn, 2, ln // tile),
        in_specs=[
            pl.BlockSpec((1, tile, qw), lambda b, j, t: (b, t, j)),
            pl.BlockSpec((1, kw, lk), lambda b, j, t: (b, j, 0)),
            pl.BlockSpec((1, lk, kw), lambda b, j, t: (b, 0, j)),
        ],
        out_specs=pl.BlockSpec((1, tile, qw), lambda b, j, t: (b, t, j)),
        out_shape=jax.ShapeDtypeStruct((bn, ln, D_Q), BF16),
        scratch_shapes=[pltpu.VMEM((lk, kw), BF16), pltpu.VMEM((lk, kw), BF16)],
        compiler_params=_cparams("parallel", "parallel", "arbitrary"),
        name="attention",
    )(q, kt_all, v_all)


def _pool_kernel(u_ref, cnt_ref, w_ref, sc_ref, o_ref, pad_ref, *, seq):
    n = seq + 2 * POOL_PAD
    zeros = jnp.zeros((POOL_PAD, D_POOL), F32)
    pad_ref[0:POOL_PAD, :] = zeros
    pad_ref[seq + POOL_PAD:n, :] = zeros
    pad_ref[POOL_PAD:seq + POOL_PAD, :] = u_ref[0]
    a = pad_ref[...]
    w2 = a + pltpu.roll(a, 1, 0)
    w4 = pltpu.roll(w2, 1, 0) + pltpu.roll(w2, n - 1, 0)
    w8 = pltpu.roll(w4, 2, 0) + pltpu.roll(w4, n - 2, 0)
    w16 = pltpu.roll(w8, 4, 0) + pltpu.roll(w8, n - 4, 0)
    lane = lax.broadcasted_iota(jnp.int32, (1, D_POOL), 1)
    g = lane // POOL_GROUP_DIM
    win = jnp.where(g == 0, w2, jnp.where(g == 1, w4, jnp.where(g == 2, w8, w16)))
    u = u_ref[0]
    pooled = win[POOL_PAD:seq + POOL_PAD, :] / cnt_ref[...] - u
    o_ref[0] = (_dot(pooled.astype(BF16), w_ref[...]) * sc_ref[...]).astype(o_ref.dtype)


def _pool(p, pool_w, pool_scale):
    bn, ln, n = p.shape
    t = jnp.arange(ln)
    cnt = jnp.concatenate(
        [jnp.broadcast_to((jnp.minimum(t + w // 2, ln) - jnp.maximum(t - w // 2, 0)).astype(F32)[:, None],
                          (ln, POOL_GROUP_DIM)) for w in POOL_WINDOWS], axis=1)
    wbd = jax.scipy.linalg.block_diag(*[pool_w[i] for i in range(pool_w.shape[0])]).astype(BF16)
    return pl.pallas_call(
        functools.partial(_pool_kernel, seq=ln),
        grid=(bn,),
        in_specs=[
            pl.BlockSpec((1, ln, D_POOL), lambda b: (b, 0, n // D_POOL - 1)),
            pl.BlockSpec((ln, D_POOL), lambda b: (0, 0)),
            pl.BlockSpec((D_POOL, D_POOL), lambda b: (0, 0)),
            pl.BlockSpec((1, D_POOL), lambda b: (0, 0)),
        ],
        out_specs=pl.BlockSpec((1, ln, D_POOL), lambda b: (b, 0, 0)),
        out_shape=jax.ShapeDtypeStruct((bn, ln, D_POOL), BF16),
        scratch_shapes=[pltpu.VMEM((ln + 2 * POOL_PAD, D_POOL), F32)],
        compiler_params=_cparams("parallel"),
        name="pool",
    )(p, cnt, wbd, pool_scale.reshape(1, -1))


def _paired_head_order():
    order = []
    per_kv = N_Q_HEADS // N_KV_HEADS
    for j in range(N_KV_HEADS // 2):
        for i in range(per_kv):
            order += [(2 * j) * per_kv + i, (2 * j + 1) * per_kv + i]
    return order


def _even_layer(x, mod, norm1_g, norm2_g, w_in, conv_w, ln_g, ln_b, w_out, w_up, w_conv, w_down, tile):
    p = _inproj(x, mod, norm1_g.reshape(1, -1), w_in.astype(BF16), tile)
    fa = _dft(p, min(512, p.shape[1]))
    bb = _convmod(p, conv_w, ln_g, ln_b)
    wo = w_out.astype(BF16)
    return _block(x, fa, bb, mod, norm2_g.reshape(1, -1), wo[:D_A], wo[D_A:], w_up, w_conv, w_down, tile)


def _odd_layer_last(x, ctx, mod, modc, norm1_g, norm2_g, w_in, q_g, k_g, pool_w, pool_scale, w_out,
                    w_up, w_conv, w_down, tile):
    heads = jnp.asarray(_paired_head_order())
    qcols = (heads[:, None] * HEAD_DIM + jnp.arange(HEAD_DIM)[None, :]).reshape(-1)
    w_in_b = w_in.astype(BF16)
    w_lat = jnp.concatenate([w_in_b[:, qcols], w_in_b[:, D_Q:]], axis=1)
    g1 = norm1_g.reshape(1, -1)
    q, kt, v, u = _inproj_qk(x, mod, g1, w_lat, q_g, k_g, tile)
    pc = _inproj(ctx, modc, g1, w_in_b[:, D_Q:D_Q + 2 * D_KV], ctx.shape[1])
    kct, vc = _kvprep_ctx(pc, k_g)
    attn = _attention(q, jnp.concatenate([kct, kt], axis=2), jnp.concatenate([vc, v], axis=1), ATTN_TILE)
    pooled = _pool(u, pool_w, pool_scale)
    wo = w_out.astype(BF16)
    return _block(x, attn, pooled, mod, norm2_g.reshape(1, -1), wo[:D_Q][qcols], wo[D_Q:],
                  w_up, w_conv, w_down, tile)


def kernel(x, c, ctx, c_ctx, w_ada, b_ada, norm1_g, norm2_g, ev_w_in, ev_conv_w, ev_ln_g, ev_ln_b, ev_w_out,
           od_w_in, od_q_g, od_k_g, od_pool_w, od_pool_scale, od_w_out, ffn_w_up, ffn_conv_w, ffn_w_down):
    depth = w_ada.shape[0]
    assert depth == 2, "even layer followed by a final odd layer"
    bn = x.shape[0]
    rows = -(-(bn + 1) // 8) * 8
    cc = jnp.concatenate([c, c_ctx[None, :], jnp.zeros((rows - bn - 1, D_MODEL), F32)], axis=0)
    mods = _ada(cc, w_ada, b_ada)
    mod = [mods[i, :bn].reshape(bn, 6, D_MODEL) for i in range(depth)]
    modc = [mods[i, bn:bn + 1].reshape(1, 6, D_MODEL) for i in range(depth)]

    tile = 512
    ev = (ev_w_in[0], ev_conv_w[0], ev_ln_g[0], ev_ln_b[0], ev_w_out[0],
          ffn_w_up[0], ffn_conv_w[0], ffn_w_down[0])
    x = _even_layer(x, mod[0], norm1_g[0], norm2_g[0], *ev, tile)
    ctx = _even_layer(ctx, modc[0], norm1_g[0], norm2_g[0], *ev, ctx.shape[1])
    return _odd_layer_last(x, ctx, mod[1], modc[1], norm1_g[1], norm2_g[1], od_w_in[0], od_q_g[0], od_k_g[0],
                           od_pool_w[0], od_pool_scale[0], od_w_out[0],
                           ffn_w_up[1], ffn_conv_w[1], ffn_w_down[1], tile)
```

```python
import functools

import jax
import jax.numpy as jnp
import numpy as np
from jax import lax
from jax.experimental import pallas as pl
from jax.experimental.pallas import tpu as pltpu

F32 = jnp.float32
BF16 = jnp.bfloat16

D_MODEL = 1024
GRID_W = 64
EPS = 1e-6

A_GROUPS = 4
A_GROUP_DIM = 128
D_A = A_GROUPS * A_GROUP_DIM
D_B = 512
CONV_WIDTH = 31
CONV_PAD = 16
CONV_BLOCK = 8

HEAD_DIM = 64
N_Q_HEADS = 12
N_KV_HEADS = 4
D_Q = N_Q_HEADS * HEAD_DIM
D_KV = N_KV_HEADS * HEAD_DIM
POOL_WINDOWS = (2, 4, 8, 16)
POOL_GROUP_DIM = 64
D_POOL = len(POOL_WINDOWS) * POOL_GROUP_DIM
POOL_PAD = 16
ROPE_THETA = 10000.0
ROPE_PAIRS = HEAD_DIM // 4
Q_SCALE = HEAD_DIM ** -0.5 * 1.4426950408889634

D_FF = 2816
FFN_CHUNK = 256
ATTN_TILE = 1024
ATTN_SUB = 512
FFN_HALO = 16

LANES = 128
SUBLANES = 8
VMEM_LIMIT = 56 * 1024 * 1024


def _cparams(*sem):
    return pltpu.CompilerParams(dimension_semantics=sem, vmem_limit_bytes=VMEM_LIMIT)


def _sigmoid(x):
    return 1.0 / (1.0 + jnp.exp(-x))


def _silu(x):
    return x * _sigmoid(x)


def _norm_mod(x, g, shift, scale):
    ms = jnp.mean(x * x, axis=-1, keepdims=True)
    y = x * lax.rsqrt(ms + EPS) * g
    return y * (1.0 + scale) + shift


def _dot(a, b):
    return jnp.dot(a, b, preferred_element_type=F32)


def _ada_kernel(cc_ref, w_ref, b_ref, o_ref):
    s = _silu(cc_ref[...])
    o_ref[0] = _dot(s.astype(BF16), w_ref[0].astype(BF16)) + b_ref[0]


def _ada(cc, w_ada, b_ada):
    depth = w_ada.shape[0]
    rows = cc.shape[0]
    nblk = w_ada.shape[2] // D_MODEL
    return pl.pallas_call(
        _ada_kernel,
        grid=(depth, nblk),
        in_specs=[
            pl.BlockSpec((rows, D_MODEL), lambda i, n: (0, 0)),
            pl.BlockSpec((1, D_MODEL, D_MODEL), lambda i, n: (i, 0, n)),
            pl.BlockSpec((1, 1, D_MODEL), lambda i, n: (i, 0, n)),
        ],
        out_specs=pl.BlockSpec((1, rows, D_MODEL), lambda i, n: (i, 0, n)),
        out_shape=jax.ShapeDtypeStruct((depth, rows, w_ada.shape[2]), F32),
        compiler_params=_cparams("parallel", "parallel"),
        name="ada",
    )(cc, w_ada, b_ada.reshape(depth, 1, -1))


def _inproj_kernel(x_ref, mod_ref, g_ref, w_ref, o_ref):
    h = _norm_mod(x_ref[0], g_ref[...], mod_ref[0, 0:1, :], mod_ref[0, 1:2, :])
    o_ref[0] = _dot(h.astype(BF16), w_ref[...])


def _inproj(x, mod, g, w, tile):
    bn, ln, _ = x.shape
    n = w.shape[1]
    per_batch = mod.shape[0] > 1
    return pl.pallas_call(
        _inproj_kernel,
        grid=(bn, ln // tile),
        in_specs=[
            pl.BlockSpec((1, tile, D_MODEL), lambda b, t: (b, t, 0)),
            pl.BlockSpec((1, 6, D_MODEL), (lambda b, t: (b, 0, 0)) if per_batch else (lambda b, t: (0, 0, 0))),
            pl.BlockSpec((1, D_MODEL), lambda b, t: (0, 0)),
            pl.BlockSpec((D_MODEL, n), lambda b, t: (0, 0)),
        ],
        out_specs=pl.BlockSpec((1, tile, n), lambda b, t: (b, t, 0)),
        out_shape=jax.ShapeDtypeStruct((bn, ln, n), F32),
        compiler_params=_cparams("parallel", "parallel"),
        name="inproj",
    )(x, mod, g, w)


def _inproj_even_kernel(x_ref, mod_ref, g_ref, w_ref, a_ref, b_ref):
    h = _norm_mod(x_ref[0], g_ref[...], mod_ref[0, 0:1, :], mod_ref[0, 1:2, :])
    p = _dot(h.astype(BF16), w_ref[...])
    a_ref[0] = p[:, :D_A].astype(BF16)
    b_ref[0] = p[:, D_A:D_A + D_B] * _sigmoid(p[:, D_A + D_B:])


def _inproj_even(x, mod, g, w, tile):
    bn, ln, _ = x.shape
    n = w.shape[1]
    per_batch = mod.shape[0] > 1
    row = lambda b, t: (b, t, 0)
    return pl.pallas_call(
        _inproj_even_kernel,
        grid=(bn, ln // tile),
        in_specs=[
            pl.BlockSpec((1, tile, D_MODEL), row),
            pl.BlockSpec((1, 6, D_MODEL), (lambda b, t: (b, 0, 0)) if per_batch else (lambda b, t: (0, 0, 0))),
            pl.BlockSpec((1, D_MODEL), lambda b, t: (0, 0)),
            pl.BlockSpec((D_MODEL, n), lambda b, t: (0, 0)),
        ],
        out_specs=[pl.BlockSpec((1, tile, D_A), row), pl.BlockSpec((1, tile, D_B), row)],
        out_shape=[jax.ShapeDtypeStruct((bn, ln, D_A), BF16), jax.ShapeDtypeStruct((bn, ln, D_B), F32)],
        compiler_params=_cparams("parallel", "parallel"),
        name="inproj_even",
    )(x, mod, g, w)


def _dft_kernel(a_ref, cs_ref, m_ref, o_ref, z_ref, *, seq, scale):
    @pl.when(pl.program_id(1) == 0)
    def _():
        for g in range(A_GROUPS):
            cols = slice(g * A_GROUP_DIM, (g + 1) * A_GROUP_DIM)
            xg = _dot(a_ref[0, :, cols], cs_ref[...])
            z_ref[0:seq, cols] = xg[:, :A_GROUP_DIM].astype(BF16)
            z_ref[seq:2 * seq, cols] = xg[:, A_GROUP_DIM:].astype(BF16)

    o_ref[0] = (_dot(m_ref[...], z_ref[...]) * scale).astype(o_ref.dtype)


def _dft_matrices(seq):
    def cos_sin(n):
        idx = np.arange(n, dtype=np.int64)
        ang = ((idx[:, None] * idx[None, :]) % n).astype(np.float64) * (2.0 * np.pi / n)
        return np.cos(ang), np.sin(ang)
    cc, sc = cos_sin(A_GROUP_DIM)
    cl, sl = cos_sin(seq)
    return (jnp.asarray(np.concatenate([cc, sc], axis=1).astype(BF16)),
            jnp.asarray(np.concatenate([cl, -sl], axis=1).astype(BF16)))


def _dft(p, seq_block):
    bn, ln, _ = p.shape
    cs, m = _dft_matrices(ln)
    scale = float(1.0 / (ln * A_GROUP_DIM) ** 0.5)
    return pl.pallas_call(
        functools.partial(_dft_kernel, seq=ln, scale=scale),
        grid=(bn, ln // seq_block),
        in_specs=[
            pl.BlockSpec((1, ln, D_A), lambda b, k: (b, 0, 0)),
            pl.BlockSpec((A_GROUP_DIM, 2 * A_GROUP_DIM), lambda b, k: (0, 0)),
            pl.BlockSpec((seq_block, 2 * ln), lambda b, k: (k, 0)),
        ],
        out_specs=pl.BlockSpec((1, seq_block, D_A), lambda b, k: (b, k, 0)),
        out_shape=jax.ShapeDtypeStruct((bn, ln, D_A), BF16),
        scratch_shapes=[pltpu.VMEM((2 * ln, D_A), BF16)],
        compiler_params=_cparams("parallel", "arbitrary"),
        name="dft",
    )(p, cs, m)


def _convmod_kernel(*refs, seq):
    ncb = D_B // LANES
    b_refs = refs[0:ncb]
    cw_ref, lg_ref, lb_ref, o_ref = refs[ncb:ncb + 4]
    e_refs, y_refs = refs[ncb + 4:2 * ncb + 4], refs[2 * ncb + 4:3 * ncb + 4]
    grp = seq // SUBLANES
    halo = CONV_PAD * SUBLANES
    sub = lax.broadcasted_iota(jnp.int32, (halo, 1), 0) % SUBLANES

    rows = CONV_BLOCK * SUBLANES
    norm_rows = 128

    for c in range(ncb):
        cols = slice(c * LANES, (c + 1) * LANES)

        def permute(i, carry, c=c):
            for j in range(SUBLANES):
                k = i * SUBLANES + j
                dst = pl.multiple_of(halo + k * SUBLANES, SUBLANES)
                e_refs[c][pl.ds(dst, SUBLANES), :] = b_refs[c][0, pl.ds(k, SUBLANES, stride=grp), :]
            return carry

        lax.fori_loop(0, grp // SUBLANES, permute, 0)
        first = e_refs[c][halo:2 * halo, :]
        lastb = e_refs[c][seq:seq + halo, :]
        e_refs[c][0:halo, :] = jnp.where(sub == 0, 0.0, pltpu.roll(lastb, 1, 0))
        e_refs[c][seq + halo:seq + 2 * halo, :] = jnp.where(sub == SUBLANES - 1, 0.0, pltpu.roll(first, halo - 1, 0))

        taps = [cw_ref[t * SUBLANES:(t + 1) * SUBLANES, cols] for t in range(CONV_WIDTH)]

        def conv(i, carry, c=c, taps=taps):
            r0 = pl.multiple_of(i * rows, rows)
            acc = [[None, None] for _ in range(CONV_BLOCK)]
            for idx in range(CONV_BLOCK + CONV_WIDTH - 1):
                src = r0 + (idx + CONV_PAD - CONV_WIDTH // 2) * SUBLANES
                xin = e_refs[c][pl.ds(src, SUBLANES), :]
                for a in range(CONV_BLOCK):
                    t = idx - a
                    if 0 <= t < CONV_WIDTH:
                        term = xin * taps[t]
                        acc[a][t % 2] = term if acc[a][t % 2] is None else acc[a][t % 2] + term
            for a in range(CONV_BLOCK):
                y_refs[c][pl.ds(r0 + a * SUBLANES, SUBLANES), :] = acc[a][0] + acc[a][1]
            return carry

        lax.fori_loop(0, grp // CONV_BLOCK, conv, 0)

    def norm(i, carry):
        r0 = pl.multiple_of(i * norm_rows, norm_rows)
        accs = [y_refs[c][pl.ds(r0, norm_rows), :] for c in range(ncb)]
        mu = jnp.sum(sum(accs), axis=-1, keepdims=True) * (1.0 / D_B)
        cens = [a - mu for a in accs]
        var = jnp.sum(sum(a * a for a in cens), axis=-1, keepdims=True) * (1.0 / D_B)
        inv = lax.rsqrt(var + EPS)
        for c in range(ncb):
            cols = slice(c * LANES, (c + 1) * LANES)
            y_refs[c][pl.ds(r0, norm_rows), :] = _silu(cens[c] * inv * lg_ref[:, cols] + lb_ref[:, cols])
        return carry

    lax.fori_loop(0, seq // norm_rows, norm, 0, unroll=2)

    gb = grp // SUBLANES
    blk = SUBLANES * SUBLANES

    def unpermute(r2, carry):
        src = pl.multiple_of(r2 * 2 * blk, 2 * blk)
        dst = pl.multiple_of(r2 * 2 * SUBLANES, 2 * SUBLANES)
        for q in range(SUBLANES):
            for c in range(ncb):
                two = [y_refs[c][pl.ds(src + h * blk + q, SUBLANES, stride=SUBLANES), :] for h in range(2)]
                o_ref[0, pl.ds(dst + q * grp, 2 * SUBLANES), c * LANES:(c + 1) * LANES] = (
                    jnp.concatenate(two, axis=0).astype(o_ref.dtype))
        return carry

    lax.fori_loop(0, gb // 2, unpermute, 0)


def _convmod(b, conv_w, ln_g, ln_b):
    bn, ln, _ = b.shape
    ncb = D_B // LANES
    cw = jnp.repeat(conv_w, SUBLANES, axis=0)
    col_spec = lambda cb: pl.BlockSpec((1, ln, LANES), lambda i: (i, 0, cb))
    return pl.pallas_call(
        functools.partial(_convmod_kernel, seq=ln),
        grid=(bn,),
        in_specs=[col_spec(c) for c in range(ncb)] + [
            pl.BlockSpec((CONV_WIDTH * SUBLANES, D_B), lambda b: (0, 0)),
            pl.BlockSpec((1, D_B), lambda b: (0, 0)),
            pl.BlockSpec((1, D_B), lambda b: (0, 0)),
        ],
        out_specs=pl.BlockSpec((1, ln, D_B), lambda b: (b, 0, 0)),
        out_shape=jax.ShapeDtypeStruct((bn, ln, D_B), BF16),
        scratch_shapes=[pltpu.VMEM((ln + 2 * CONV_PAD * SUBLANES, LANES), F32) for _ in range(ncb)]
                       + [pltpu.VMEM((ln, LANES), F32) for _ in range(ncb)],
        compiler_params=_cparams("parallel"),
        name="convmod",
    )(*([b] * ncb), cw, ln_g.reshape(1, -1), ln_b.reshape(1, -1))


def _block_kernel(xp_ref, x_ref, xn_ref, ap_ref, a_ref, an_ref, bp_ref, b_ref, bn_ref, mod_ref, g_ref,
                  w1_ref, w2_ref, wu_ref, cw_ref, wd_ref, o_ref, h_ref, x1_ref, acc_ref, *, tile):
    t = pl.program_id(1)
    last = pl.num_programs(1) - 1
    g = g_ref[...]
    gate1 = mod_ref[0, 2:3, :]
    shift, scale, gate2 = mod_ref[0, 3:4, :], mod_ref[0, 4:5, :], mod_ref[0, 5:6, :]
    ext = tile + 2 * FFN_HALO

    def mixed(xr, ar, br):
        return xr[0] + gate1 * (_dot(ar[0], w1_ref[...]) + _dot(br[0], w2_ref[...]))

    x1_ref[...] = mixed(x_ref, a_ref, b_ref)
    hp = _norm_mod(mixed(xp_ref, ap_ref, bp_ref), g, shift, scale)
    hn = _norm_mod(mixed(xn_ref, an_ref, bn_ref), g, shift, scale)
    h_ref[0:FFN_HALO, :] = jnp.where(t > 0, hp, 0.0).astype(BF16)
    h_ref[FFN_HALO:FFN_HALO + tile, :] = _norm_mod(x1_ref[...], g, shift, scale).astype(BF16)
    h_ref[FFN_HALO + tile:ext, :] = jnp.where(t < last, hn, 0.0).astype(BF16)
    acc_ref[...] = jnp.zeros_like(acc_ref)

    def conv3(up, cw):
        prev = pltpu.roll(up, 1, 0)
        nxt = pltpu.roll(up, ext - 1, 0)
        out = prev * cw[0:1, :] + up * cw[1:2, :] + nxt * cw[2:3, :]
        return out[FFN_HALO:FFN_HALO + tile, :]

    def chunk(j, base=0):
        return slice(base + j * FFN_CHUNK, base + (j + 1) * FFN_CHUNK)

    def up(j):
        hx = h_ref[...]
        return _dot(hx, wu_ref[:, chunk(j)]), _dot(hx, wu_ref[:, chunk(j, D_FF)])

    nch = D_FF // FFN_CHUNK
    nxt = up(0)
    acts = []
    for j in range(nch):
        cur = nxt
        if j + 1 < nch:
            nxt = up(j + 1)
        act = _silu(conv3(cur[0], cw_ref[:, chunk(j)])) * conv3(cur[1], cw_ref[:, chunk(j, D_FF)])
        acts.append(act.astype(BF16))
        if len(acts) == 2 or j + 1 == nch:
            lo = (j + 1 - len(acts)) * FFN_CHUNK
            acc_ref[...] += _dot(jnp.concatenate(acts, axis=1), wd_ref[lo:(j + 1) * FFN_CHUNK, :])
            acts = []
    o_ref[0] = x1_ref[...] + gate2 * acc_ref[...]


def _block(x, y1, y2, mod, g, w1, w2, w_up, w_conv, w_down, tile):
    bn, ln, _ = x.shape
    d1, d2 = y1.shape[2], y2.shape[2]
    per_batch = mod.shape[0] > 1
    hb = tile // FFN_HALO
    nhb = ln // FFN_HALO
    wc = jnp.pad(w_conv, ((0, SUBLANES - w_conv.shape[0]), (0, 0)))
    const2 = lambda b, t: (0, 0)
    const3 = lambda b, t: (0, 0, 0)
    prev = lambda b, t: (b, jnp.maximum(t * hb - 1, 0), 0)
    main = lambda b, t: (b, t, 0)
    nxt = lambda b, t: (b, jnp.minimum((t + 1) * hb, nhb - 1), 0)

    def rows3(d):
        return [pl.BlockSpec((1, FFN_HALO, d), prev), pl.BlockSpec((1, tile, d), main),
                pl.BlockSpec((1, FFN_HALO, d), nxt)]

    return pl.pallas_call(
        functools.partial(_block_kernel, tile=tile),
        grid=(bn, ln // tile),
        in_specs=rows3(D_MODEL) + rows3(d1) + rows3(d2) + [
            pl.BlockSpec((1, 6, D_MODEL), (lambda b, t: (b, 0, 0)) if per_batch else const3),
            pl.BlockSpec((1, D_MODEL), const2),
            pl.BlockSpec((d1, D_MODEL), const2),
            pl.BlockSpec((d2, D_MODEL), const2),
            pl.BlockSpec((D_MODEL, 2 * D_FF), const2),
            pl.BlockSpec((SUBLANES, 2 * D_FF), const2),
            pl.BlockSpec((D_FF, D_MODEL), const2),
        ],
        out_specs=pl.BlockSpec((1, tile, D_MODEL), main),
        out_shape=jax.ShapeDtypeStruct((bn, ln, D_MODEL), F32),
        scratch_shapes=[pltpu.VMEM((tile + 2 * FFN_HALO, D_MODEL), BF16),
                        pltpu.VMEM((tile, D_MODEL), F32),
                        pltpu.VMEM((tile, D_MODEL), F32)],
        compiler_params=_cparams("parallel", "parallel"),
        name="block",
    )(x, x, x, y1, y1, y1, y2, y2, y2, mod, g, w1, w2, w_up.astype(BF16), wc, w_down.astype(BF16))


def _head_norm(xb, ones_ref, gain):
    sq = xb * xb
    hi = sq.astype(BF16)
    lo = (sq - hi.astype(F32)).astype(BF16)
    ss = _dot(hi, ones_ref[...]) + _dot(lo, ones_ref[...])
    return xb * lax.rsqrt(ss * (1.0 / HEAD_DIM) + EPS) * gain


def _rope(y, cos, sin, first):
    partner = jnp.where(first, pltpu.roll(y, LANES - ROPE_PAIRS, 1), pltpu.roll(y, ROPE_PAIRS, 1))
    return y * cos + partner * sin


def _inproj_qk_kernel(x_ref, mod_ref, g_ref, w_ref, cos_ref, sin_ref, gq_ref, gk_ref, ones_ref,
                      q_ref, kt_ref, v_ref, u_ref):
    h = _norm_mod(x_ref[0], g_ref[...], mod_ref[0, 0:1, :], mod_ref[0, 1:2, :])
    p = _dot(h.astype(BF16), w_ref[...])
    lane = lax.broadcasted_iota(jnp.int32, (1, LANES), 1)
    first = (lane % (2 * ROPE_PAIRS)) < ROPE_PAIRS
    cos, sin = cos_ref[...], sin_ref[...]
    for c in range(D_Q // LANES):
        cols = slice(c * LANES, (c + 1) * LANES)
        y = _head_norm(p[:, cols], ones_ref, gq_ref[...])
        q_ref[0, :, cols] = (_rope(y, cos, sin, first) * Q_SCALE).astype(BF16)
    for c in range(D_KV // LANES):
        src = slice(D_Q + c * LANES, D_Q + (c + 1) * LANES)
        y = _head_norm(p[:, src], ones_ref, gk_ref[...])
        kt_ref[0, c * LANES:(c + 1) * LANES, :] = _rope(y, cos, sin, first).T.astype(BF16)
    v_ref[0] = p[:, D_Q + D_KV:D_Q + 2 * D_KV].astype(BF16)
    u_ref[0] = p[:, D_Q + 2 * D_KV:]


def _kvprep_ctx_kernel(p_ref, gk_ref, ones_ref, kt_ref, v_ref):
    for c in range(D_KV // LANES):
        cols = slice(c * LANES, (c + 1) * LANES)
        kt_ref[0, cols, :] = _head_norm(p_ref[0, :, cols], ones_ref, gk_ref[...]).T.astype(BF16)
    v_ref[0] = p_ref[0, :, D_KV:2 * D_KV].astype(BF16)


def _group_ones():
    r = np.arange(LANES) // HEAD_DIM
    return jnp.asarray(r[:, None] == r[None, :], BF16)


def _rope_tables(seq):
    t = np.arange(seq)
    freqs = ROPE_THETA ** (-np.arange(ROPE_PAIRS, dtype=np.float64) / ROPE_PAIRS)
    ang_r = (t // GRID_W).astype(np.float64)[:, None] * freqs
    ang_c = (t % GRID_W).astype(np.float64)[:, None] * freqs
    cos = np.concatenate([np.cos(ang_r)] * 2 + [np.cos(ang_c)] * 2, axis=1)
    sin = np.concatenate([-np.sin(ang_r), np.sin(ang_r), -np.sin(ang_c), np.sin(ang_c)], axis=1)
    reps = (1, LANES // HEAD_DIM)
    return jnp.asarray(np.tile(cos, reps), F32), jnp.asarray(np.tile(sin, reps), F32)


def _inproj_qk(x, mod, g, w, q_g, k_g, tile):
    bn, ln, _ = x.shape
    n = w.shape[1]
    cos, sin = _rope_tables(ln)
    gq = jnp.tile(q_g, LANES // HEAD_DIM).reshape(1, LANES)
    gk = jnp.tile(k_g, LANES // HEAD_DIM).reshape(1, LANES)
    row = lambda b, t: (b, t, 0)
    tab = lambda b, t: (t, 0)
    const = lambda b, t: (0, 0)
    return pl.pallas_call(
        _inproj_qk_kernel,
        grid=(bn, ln // tile),
        in_specs=[
            pl.BlockSpec((1, tile, D_MODEL), row),
            pl.BlockSpec((1, 6, D_MODEL), lambda b, t: (b, 0, 0)),
            pl.BlockSpec((1, D_MODEL), const),
            pl.BlockSpec((D_MODEL, n), const),
            pl.BlockSpec((tile, LANES), tab),
            pl.BlockSpec((tile, LANES), tab),
            pl.BlockSpec((1, LANES), const),
            pl.BlockSpec((1, LANES), const),
            pl.BlockSpec((LANES, LANES), const),
        ],
        out_specs=[pl.BlockSpec((1, tile, D_Q), row),
                   pl.BlockSpec((1, D_KV, tile), lambda b, t: (b, 0, t)),
                   pl.BlockSpec((1, tile, D_KV), row),
                   pl.BlockSpec((1, tile, D_POOL), row)],
        out_shape=[jax.ShapeDtypeStruct((bn, ln, D_Q), BF16),
                   jax.ShapeDtypeStruct((bn, D_KV, ln), BF16),
                   jax.ShapeDtypeStruct((bn, ln, D_KV), BF16),
                   jax.ShapeDtypeStruct((bn, ln, D_POOL), F32)],
        compiler_params=_cparams("parallel", "parallel"),
        name="inproj_qk",
    )(x, mod, g, w, cos, sin, gq, gk, _group_ones())


def _kvprep_ctx(pc, k_g):
    bn, lc, n = pc.shape
    gk = jnp.tile(k_g, LANES // HEAD_DIM).reshape(1, LANES)
    return pl.pallas_call(
        _kvprep_ctx_kernel,
        grid=(bn,),
        in_specs=[
            pl.BlockSpec((1, lc, n), lambda b: (b, 0, 0)),
            pl.BlockSpec((1, LANES), lambda b: (0, 0)),
            pl.BlockSpec((LANES, LANES), lambda b: (0, 0)),
        ],
        out_specs=[pl.BlockSpec((1, D_KV, lc), lambda b: (b, 0, 0)),
                   pl.BlockSpec((1, lc, D_KV), lambda b: (b, 0, 0))],
        out_shape=[jax.ShapeDtypeStruct((bn, D_KV, lc), BF16),
                   jax.ShapeDtypeStruct((bn, lc, D_KV), BF16)],
        compiler_params=_cparams("parallel"),
        name="kvprep_ctx",
    )(pc, gk, _group_ones())


def _attn_kernel(q_ref, kt_ref, v_ref, o_ref, vlo_ref, vhi_ref):
    low = lax.broadcasted_iota(jnp.int32, (1, LANES), 1) < HEAD_DIM

    @pl.when(pl.program_id(2) == 0)
    def _():
        v = v_ref[0]
        one = jnp.ones_like(v)
        vlo_ref[...] = jnp.where(low, v, one)
        vhi_ref[...] = jnp.where(low, one, v)

    kt = kt_ref[0]
    for r0 in range(0, q_ref.shape[1], ATTN_SUB):
        rows = slice(r0, r0 + ATTN_SUB)
        for c in range(q_ref.shape[2] // LANES):
            cols = slice(c * LANES, (c + 1) * LANES)
            qp = q_ref[0, rows, cols]
            zero = jnp.zeros_like(qp)
            halves = []
            for qm, vm_ref in ((jnp.where(low, qp, zero), vlo_ref), (jnp.where(low, zero, qp), vhi_ref)):
                s = _dot(qm, kt)
                e = jnp.exp2(s - jnp.max(s, axis=-1, keepdims=True))
                o = _dot(e.astype(BF16), vm_ref[...])
                halves.append(o / pltpu.roll(o, HEAD_DIM, 1))
            o_ref[0, rows, cols] = jnp.where(low, halves[0], halves[1]).astype(o_ref.dtype)


def _attention(q, kt_all, v_all, tile):
    bn, ln, _ = q.shape
    lk = v_all.shape[1]
    qw = D_Q // 2
    kw = D_KV // 2
    return pl.pallas_call(
        _attn_kernel,
        grid=(bn, 2, ln // tile),
        in_specs=[
            pl.BlockSpec((1, tile, qw), lambda b, j, t: (b, t, j)),
            pl.BlockSpec((1, kw, lk), lambda b, j, t: (b, j, 0)),
            pl.BlockSpec((1, lk, kw), lambda b, j, t: (b, 0, j)),
        ],
        out_specs=pl.BlockSpec((1, tile, qw), lambda b, j, t: (b, t, j)),
        out_shape=jax.ShapeDtypeStruct((bn, ln, D_Q), BF16),
        scratch_shapes=[pltpu.VMEM((lk, kw), BF16), pltpu.VMEM((lk, kw), BF16)],
        compiler_params=_cparams("parallel", "parallel", "arbitrary"),
        name="attention",
    )(q, kt_all, v_all)


def _pool_kernel(u_ref, cnt_ref, w_ref, sc_ref, o_ref, pad_ref, *, seq):
    n = seq + 2 * POOL_PAD
    zeros = jnp.zeros((POOL_PAD, D_POOL), F32)
    pad_ref[0:POOL_PAD, :] = zeros
    pad_ref[seq + POOL_PAD:n, :] = zeros
    pad_ref[POOL_PAD:seq + POOL_PAD, :] = u_ref[0]
    a = pad_ref[...]
    w2 = a + pltpu.roll(a, 1, 0)
    w4 = pltpu.roll(w2, 1, 0) + pltpu.roll(w2, n - 1, 0)
    w8 = pltpu.roll(w4, 2, 0) + pltpu.roll(w4, n - 2, 0)
    w16 = pltpu.roll(w8, 4, 0) + pltpu.roll(w8, n - 4, 0)
    lane = lax.broadcasted_iota(jnp.int32, (1, D_POOL), 1)
    g = lane // POOL_GROUP_DIM
    win = jnp.where(g == 0, w2, jnp.where(g == 1, w4, jnp.where(g == 2, w8, w16)))
    u = u_ref[0]
    pooled = win[POOL_PAD:seq + POOL_PAD, :] / cnt_ref[...] - u
    o_ref[0] = (_dot(pooled.astype(BF16), w_ref[...]) * sc_ref[...]).astype(o_ref.dtype)


def _pool(p, pool_w, pool_scale):
    bn, ln, n = p.shape
    t = np.arange(ln)
    cnt = jnp.asarray(np.concatenate(
        [np.broadcast_to((np.minimum(t + w // 2, ln) - np.maximum(t - w // 2, 0))[:, None], (ln, POOL_GROUP_DIM))
         for w in POOL_WINDOWS], axis=1), F32)
    wbd = jax.scipy.linalg.block_diag(*[pool_w[i] for i in range(pool_w.shape[0])]).astype(BF16)
    return pl.pallas_call(
        functools.partial(_pool_kernel, seq=ln),
        grid=(bn,),
        in_specs=[
            pl.BlockSpec((1, ln, D_POOL), lambda b: (b, 0, n // D_POOL - 1)),
            pl.BlockSpec((ln, D_POOL), lambda b: (0, 0)),
            pl.BlockSpec((D_POOL, D_POOL), lambda b: (0, 0)),
            pl.BlockSpec((1, D_POOL), lambda b: (0, 0)),
        ],
        out_specs=pl.BlockSpec((1, ln, D_POOL), lambda b: (b, 0, 0)),
        out_shape=jax.ShapeDtypeStruct((bn, ln, D_POOL), BF16),
        scratch_shapes=[pltpu.VMEM((ln + 2 * POOL_PAD, D_POOL), F32)],
        compiler_params=_cparams("parallel"),
        name="pool",
    )(p, cnt, wbd, pool_scale.reshape(1, -1))


def _paired_head_order():
    order = []
    per_kv = N_Q_HEADS // N_KV_HEADS
    for j in range(N_KV_HEADS // 2):
        for i in range(per_kv):
            order += [(2 * j) * per_kv + i, (2 * j + 1) * per_kv + i]
    return order


def _even_layer(x, mod, norm1_g, norm2_g, w_in, conv_w, ln_g, ln_b, w_out, w_up, w_conv, w_down, tile):
    a, b = _inproj_even(x, mod, norm1_g.reshape(1, -1), w_in.astype(BF16), tile)
    fa = _dft(a, min(512, a.shape[1]))
    bb = _convmod(b, conv_w, ln_g, ln_b)
    wo = w_out.astype(BF16)
    return _block(x, fa, bb, mod, norm2_g.reshape(1, -1), wo[:D_A], wo[D_A:], w_up, w_conv, w_down, tile)


def _odd_layer_last(x, ctx, mod, modc, norm1_g, norm2_g, w_in, q_g, k_g, pool_w, pool_scale, w_out,
                    w_up, w_conv, w_down, tile):
    heads = jnp.asarray(_paired_head_order())
    qcols = (heads[:, None] * HEAD_DIM + jnp.arange(HEAD_DIM)[None, :]).reshape(-1)
    w_in_b = w_in.astype(BF16)
    w_lat = jnp.concatenate([w_in_b[:, qcols], w_in_b[:, D_Q:]], axis=1)
    g1 = norm1_g.reshape(1, -1)
    q, kt, v, u = _inproj_qk(x, mod, g1, w_lat, q_g, k_g, tile)
    pc = _inproj(ctx, modc, g1, w_in_b[:, D_Q:D_Q + 2 * D_KV], ctx.shape[1])
    kct, vc = _kvprep_ctx(pc, k_g)
    attn = _attention(q, jnp.concatenate([kct, kt], axis=2), jnp.concatenate([vc, v], axis=1), ATTN_TILE)
    pooled = _pool(u, pool_w, pool_scale)
    wo = w_out.astype(BF16)
    return _block(x, attn, pooled, mod, norm2_g.reshape(1, -1), wo[:D_Q][qcols], wo[D_Q:],
                  w_up, w_conv, w_down, tile)


def kernel(x, c, ctx, c_ctx, w_ada, b_ada, norm1_g, norm2_g, ev_w_in, ev_conv_w, ev_ln_g, ev_ln_b, ev_w_out,
           od_w_in, od_q_g, od_k_g, od_pool_w, od_pool_scale, od_w_out, ffn_w_up, ffn_conv_w, ffn_w_down):
    depth = w_ada.shape[0]
    assert depth == 2, "even layer followed by a final odd layer"
    bn = x.shape[0]
    rows = -(-(bn + 1) // 8) * 8
    cc = jnp.concatenate([c, c_ctx[None, :], jnp.zeros((rows - bn - 1, D_MODEL), F32)], axis=0)
    mods = _ada(cc, w_ada, b_ada)
    mod = [mods[i, :bn].reshape(bn, 6, D_MODEL) for i in range(depth)]
    modc = [mods[i, bn:bn + 1].reshape(1, 6, D_MODEL) for i in range(depth)]

    tile = 512
    ev = (ev_w_in[0], ev_conv_w[0], ev_ln_g[0], ev_ln_b[0], ev_w_out[0],
          ffn_w_up[0], ffn_conv_w[0], ffn_w_down[0])
    x = _even_layer(x, mod[0], norm1_g[0], norm2_g[0], *ev, tile)
    ctx = _even_layer(ctx, modc[0], norm1_g[0], norm2_g[0], *ev, ctx.shape[1])
    return _odd_layer_last(x, ctx, mod[1], modc[1], norm1_g[1], norm2_g[1], od_w_in[0], od_q_g[0], od_k_g[0],
                           od_pool_w[0], od_pool_scale[0], od_w_out[0],
                           ffn_w_up[1], ffn_conv_w[1], ffn_w_down[1], tile)
```

```python
import functools

import jax
import jax.numpy as jnp
import numpy as np
from jax import lax
from jax.experimental import pallas as pl
from jax.experimental.pallas import tpu as pltpu

F32 = jnp.float32
BF16 = jnp.bfloat16

D_MODEL = 1024
GRID_W = 64
EPS = 1e-6

A_GROUPS = 4
A_GROUP_DIM = 128
D_A = A_GROUPS * A_GROUP_DIM
D_B = 512
CONV_WIDTH = 31
CONV_PAD = 16
CONV_BLOCK = 8

HEAD_DIM = 64
N_Q_HEADS = 12
N_KV_HEADS = 4
D_Q = N_Q_HEADS * HEAD_DIM
D_KV = N_KV_HEADS * HEAD_DIM
POOL_WINDOWS = (2, 4, 8, 16)
POOL_GROUP_DIM = 64
D_POOL = len(POOL_WINDOWS) * POOL_GROUP_DIM
POOL_PAD = 16
ROPE_THETA = 10000.0
ROPE_PAIRS = HEAD_DIM // 4
Q_SCALE = HEAD_DIM ** -0.5 * 1.4426950408889634

D_FF = 2816
FFN_CHUNK = 256
ATTN_TILE = 1024
ATTN_SUB = 512
FFN_HALO = 16

LANES = 128
SUBLANES = 8
VMEM_LIMIT = 56 * 1024 * 1024


def _cparams(*sem):
    return pltpu.CompilerParams(dimension_semantics=sem, vmem_limit_bytes=VMEM_LIMIT)


def _sigmoid(x):
    return 1.0 / (1.0 + jnp.exp(-x))


def _silu(x):
    return x * _sigmoid(x)


def _norm_mod(x, g, shift, scale):
    ms = jnp.mean(x * x, axis=-1, keepdims=True)
    y = x * lax.rsqrt(ms + EPS) * g
    return y * (1.0 + scale) + shift


def _dot(a, b):
    return jnp.dot(a, b, preferred_element_type=F32)


def _ada_kernel(cc_ref, w_ref, b_ref, o_ref):
    s = _silu(cc_ref[...])
    o_ref[0] = _dot(s.astype(BF16), w_ref[0].astype(BF16)) + b_ref[0]


def _ada(cc, w_ada, b_ada):
    depth = w_ada.shape[0]
    rows = cc.shape[0]
    nblk = w_ada.shape[2] // D_MODEL
    return pl.pallas_call(
        _ada_kernel,
        grid=(depth, nblk),
        in_specs=[
            pl.BlockSpec((rows, D_MODEL), lambda i, n: (0, 0)),
            pl.BlockSpec((1, D_MODEL, D_MODEL), lambda i, n: (i, 0, n)),
            pl.BlockSpec((1, 1, D_MODEL), lambda i, n: (i, 0, n)),
        ],
        out_specs=pl.BlockSpec((1, rows, D_MODEL), lambda i, n: (i, 0, n)),
        out_shape=jax.ShapeDtypeStruct((depth, rows, w_ada.shape[2]), F32),
        compiler_params=_cparams("parallel", "parallel"),
        name="ada",
    )(cc, w_ada, b_ada.reshape(depth, 1, -1))


def _inproj_kernel(x_ref, mod_ref, g_ref, w_ref, o_ref):
    h = _norm_mod(x_ref[0], g_ref[...], mod_ref[0, 0:1, :], mod_ref[0, 1:2, :])
    o_ref[0] = _dot(h.astype(BF16), w_ref[...])


def _inproj(x, mod, g, w, tile):
    bn, ln, _ = x.shape
    n = w.shape[1]
    per_batch = mod.shape[0] > 1
    return pl.pallas_call(
        _inproj_kernel,
        grid=(bn, ln // tile),
        in_specs=[
            pl.BlockSpec((1, tile, D_MODEL), lambda b, t: (b, t, 0)),
            pl.BlockSpec((1, 6, D_MODEL), (lambda b, t: (b, 0, 0)) if per_batch else (lambda b, t: (0, 0, 0))),
            pl.BlockSpec((1, D_MODEL), lambda b, t: (0, 0)),
            pl.BlockSpec((D_MODEL, n), lambda b, t: (0, 0)),
        ],
        out_specs=pl.BlockSpec((1, tile, n), lambda b, t: (b, t, 0)),
        out_shape=jax.ShapeDtypeStruct((bn, ln, n), F32),
        compiler_params=_cparams("parallel", "parallel"),
        name="inproj",
    )(x, mod, g, w)


def _inproj_even_kernel(x_ref, mod_ref, g_ref, w_ref, a_ref, b_ref):
    h = _norm_mod(x_ref[0], g_ref[...], mod_ref[0, 0:1, :], mod_ref[0, 1:2, :])
    p = _dot(h.astype(BF16), w_ref[...])
    a_ref[0] = p[:, :D_A].astype(BF16)
    b_ref[0] = p[:, D_A:D_A + D_B] * _sigmoid(p[:, D_A + D_B:])


def _inproj_even(x, mod, g, w, tile):
    bn, ln, _ = x.shape
    n = w.shape[1]
    per_batch = mod.shape[0] > 1
    row = lambda b, t: (b, t, 0)
    return pl.pallas_call(
        _inproj_even_kernel,
        grid=(bn, ln // tile),
        in_specs=[
            pl.BlockSpec((1, tile, D_MODEL), row),
            pl.BlockSpec((1, 6, D_MODEL), (lambda b, t: (b, 0, 0)) if per_batch else (lambda b, t: (0, 0, 0))),
            pl.BlockSpec((1, D_MODEL), lambda b, t: (0, 0)),
            pl.BlockSpec((D_MODEL, n), lambda b, t: (0, 0)),
        ],
        out_specs=[pl.BlockSpec((1, tile, D_A), row), pl.BlockSpec((1, tile, D_B), row)],
        out_shape=[jax.ShapeDtypeStruct((bn, ln, D_A), BF16), jax.ShapeDtypeStruct((bn, ln, D_B), F32)],
        compiler_params=_cparams("parallel", "parallel"),
        name="inproj_even",
    )(x, mod, g, w)


def _dft_kernel(a_ref, cs_ref, m_ref, o_ref, z_ref, *, seq, scale):
    @pl.when(pl.program_id(1) == 0)
    def _():
        for g in range(A_GROUPS):
            cols = slice(g * A_GROUP_DIM, (g + 1) * A_GROUP_DIM)
            xg = _dot(a_ref[0, :, cols], cs_ref[...])
            z_ref[0:seq, cols] = xg[:, :A_GROUP_DIM].astype(BF16)
            z_ref[seq:2 * seq, cols] = xg[:, A_GROUP_DIM:].astype(BF16)

    o_ref[0] = (_dot(m_ref[...], z_ref[...]) * scale).astype(o_ref.dtype)


def _dft_matrices(seq):
    def cos_sin(n):
        idx = np.arange(n, dtype=np.int64)
        ang = ((idx[:, None] * idx[None, :]) % n).astype(np.float64) * (2.0 * np.pi / n)
        return np.cos(ang), np.sin(ang)
    cc, sc = cos_sin(A_GROUP_DIM)
    cl, sl = cos_sin(seq)
    return (jnp.asarray(np.concatenate([cc, sc], axis=1).astype(BF16)),
            jnp.asarray(np.concatenate([cl, -sl], axis=1).astype(BF16)))


def _dft(p, seq_block):
    bn, ln, _ = p.shape
    cs, m = _dft_matrices(ln)
    scale = float(1.0 / (ln * A_GROUP_DIM) ** 0.5)
    return pl.pallas_call(
        functools.partial(_dft_kernel, seq=ln, scale=scale),
        grid=(bn, ln // seq_block),
        in_specs=[
            pl.BlockSpec((1, ln, D_A), lambda b, k: (b, 0, 0)),
            pl.BlockSpec((A_GROUP_DIM, 2 * A_GROUP_DIM), lambda b, k: (0, 0)),
            pl.BlockSpec((seq_block, 2 * ln), lambda b, k: (k, 0)),
        ],
        out_specs=pl.BlockSpec((1, seq_block, D_A), lambda b, k: (b, k, 0)),
        out_shape=jax.ShapeDtypeStruct((bn, ln, D_A), BF16),
        scratch_shapes=[pltpu.VMEM((2 * ln, D_A), BF16)],
        compiler_params=_cparams("parallel", "arbitrary"),
        name="dft",
    )(p, cs, m)


def _convmod_kernel(*refs, seq):
    ncb = D_B // LANES
    b_refs = refs[0:ncb]
    cw_ref, lg_ref, lb_ref, o_ref = refs[ncb:ncb + 4]
    e_refs, y_refs = refs[ncb + 4:2 * ncb + 4], refs[2 * ncb + 4:3 * ncb + 4]
    grp = seq // SUBLANES
    halo = CONV_PAD * SUBLANES
    sub = lax.broadcasted_iota(jnp.int32, (halo, 1), 0) % SUBLANES

    rows = CONV_BLOCK * SUBLANES
    norm_rows = 128

    for c in range(ncb):
        cols = slice(c * LANES, (c + 1) * LANES)

        def permute(i, carry, c=c):
            for j in range(SUBLANES):
                k = i * SUBLANES + j
                dst = pl.multiple_of(halo + k * SUBLANES, SUBLANES)
                e_refs[c][pl.ds(dst, SUBLANES), :] = b_refs[c][0, pl.ds(k, SUBLANES, stride=grp), :]
            return carry

        lax.fori_loop(0, grp // SUBLANES, permute, 0)
        first = e_refs[c][halo:2 * halo, :]
        lastb = e_refs[c][seq:seq + halo, :]
        e_refs[c][0:halo, :] = jnp.where(sub == 0, 0.0, pltpu.roll(lastb, 1, 0))
        e_refs[c][seq + halo:seq + 2 * halo, :] = jnp.where(sub == SUBLANES - 1, 0.0, pltpu.roll(first, halo - 1, 0))

        taps = [cw_ref[t * SUBLANES:(t + 1) * SUBLANES, cols] for t in range(CONV_WIDTH)]

        def conv(i, carry, c=c, taps=taps):
            r0 = pl.multiple_of(i * rows, rows)
            acc = [[None, None] for _ in range(CONV_BLOCK)]
            for idx in range(CONV_BLOCK + CONV_WIDTH - 1):
                src = r0 + (idx + CONV_PAD - CONV_WIDTH // 2) * SUBLANES
                xin = e_refs[c][pl.ds(src, SUBLANES), :]
                for a in range(CONV_BLOCK):
                    t = idx - a
                    if 0 <= t < CONV_WIDTH:
                        term = xin * taps[t]
                        acc[a][t % 2] = term if acc[a][t % 2] is None else acc[a][t % 2] + term
            for a in range(CONV_BLOCK):
                y_refs[c][pl.ds(r0 + a * SUBLANES, SUBLANES), :] = acc[a][0] + acc[a][1]
            return carry

        lax.fori_loop(0, grp // CONV_BLOCK, conv, 0)

    def norm(i, carry):
        r0 = pl.multiple_of(i * norm_rows, norm_rows)
        accs = [y_refs[c][pl.ds(r0, norm_rows), :] for c in range(ncb)]
        mu = jnp.sum(sum(accs), axis=-1, keepdims=True) * (1.0 / D_B)
        cens = [a - mu for a in accs]
        var = jnp.sum(sum(a * a for a in cens), axis=-1, keepdims=True) * (1.0 / D_B)
        inv = lax.rsqrt(var + EPS)
        for c in range(ncb):
            cols = slice(c * LANES, (c + 1) * LANES)
            y_refs[c][pl.ds(r0, norm_rows), :] = _silu(cens[c] * inv * lg_ref[:, cols] + lb_ref[:, cols])
        return carry

    lax.fori_loop(0, seq // norm_rows, norm, 0, unroll=2)

    gb = grp // SUBLANES
    blk = SUBLANES * SUBLANES

    def unpermute(r2, carry):
        src = pl.multiple_of(r2 * 2 * blk, 2 * blk)
        dst = pl.multiple_of(r2 * 2 * SUBLANES, 2 * SUBLANES)
        for q in range(SUBLANES):
            for c in range(ncb):
                two = [y_refs[c][pl.ds(src + h * blk + q, SUBLANES, stride=SUBLANES), :] for h in range(2)]
                o_ref[0, pl.ds(dst + q * grp, 2 * SUBLANES), c * LANES:(c + 1) * LANES] = (
                    jnp.concatenate(two, axis=0).astype(o_ref.dtype))
        return carry

    lax.fori_loop(0, gb // 2, unpermute, 0)


def _convmod(b, conv_w, ln_g, ln_b):
    bn, ln, _ = b.shape
    ncb = D_B // LANES
    cw = jnp.repeat(conv_w, SUBLANES, axis=0)
    col_spec = lambda cb: pl.BlockSpec((1, ln, LANES), lambda i: (i, 0, cb))
    return pl.pallas_call(
        functools.partial(_convmod_kernel, seq=ln),
        grid=(bn,),
        in_specs=[col_spec(c) for c in range(ncb)] + [
            pl.BlockSpec((CONV_WIDTH * SUBLANES, D_B), lambda b: (0, 0)),
            pl.BlockSpec((1, D_B), lambda b: (0, 0)),
            pl.BlockSpec((1, D_B), lambda b: (0, 0)),
        ],
        out_specs=pl.BlockSpec((1, ln, D_B), lambda b: (b, 0, 0)),
        out_shape=jax.ShapeDtypeStruct((bn, ln, D_B), BF16),
        scratch_shapes=[pltpu.VMEM((ln + 2 * CONV_PAD * SUBLANES, LANES), F32) for _ in range(ncb)]
                       + [pltpu.VMEM((ln, LANES), F32) for _ in range(ncb)],
        compiler_params=_cparams("parallel"),
        name="convmod",
    )(*([b] * ncb), cw, ln_g.reshape(1, -1), ln_b.reshape(1, -1))


def _block_kernel(xp_ref, x_ref, xn_ref, ap_ref, a_ref, an_ref, bp_ref, b_ref, bn_ref, mod_ref, g_ref,
                  w1_ref, w2_ref, wu_ref, cw_ref, wd_ref, o_ref, h_ref, x1_ref, act_ref, *, tile):
    t = pl.program_id(1)
    last = pl.num_programs(1) - 1
    g = g_ref[...]
    gate1 = mod_ref[0, 2:3, :]
    shift, scale, gate2 = mod_ref[0, 3:4, :], mod_ref[0, 4:5, :], mod_ref[0, 5:6, :]
    ext = tile + 2 * FFN_HALO

    def mixed(xr, ar, br):
        return xr[0] + gate1 * (_dot(ar[0], w1_ref[...]) + _dot(br[0], w2_ref[...]))

    x1_ref[...] = mixed(x_ref, a_ref, b_ref)
    hp = _norm_mod(mixed(xp_ref, ap_ref, bp_ref), g, shift, scale)
    hn = _norm_mod(mixed(xn_ref, an_ref, bn_ref), g, shift, scale)
    h_ref[0:FFN_HALO, :] = jnp.where(t > 0, hp, 0.0).astype(BF16)
    h_ref[FFN_HALO:FFN_HALO + tile, :] = _norm_mod(x1_ref[...], g, shift, scale).astype(BF16)
    h_ref[FFN_HALO + tile:ext, :] = jnp.where(t < last, hn, 0.0).astype(BF16)

    def conv3(up, cw):
        rows = slice(FFN_HALO, FFN_HALO + tile)
        prev = pltpu.roll(up, 1, 0)[rows, :]
        nxt = pltpu.roll(up, ext - 1, 0)[rows, :]
        return prev * cw[0:1, :] + up[rows, :] * cw[1:2, :] + nxt * cw[2:3, :]

    hx = h_ref[...]
    for j in range(D_FF // FFN_CHUNK):
        gcols = slice(j * FFN_CHUNK, (j + 1) * FFN_CHUNK)
        vcols = slice(D_FF + j * FFN_CHUNK, D_FF + (j + 1) * FFN_CHUNK)
        ug = conv3(_dot(hx, wu_ref[:, gcols]), cw_ref[:, gcols])
        uv = conv3(_dot(hx, wu_ref[:, vcols]), cw_ref[:, vcols])
        act_ref[:, gcols] = (_silu(ug) * uv).astype(BF16)
    o_ref[0] = x1_ref[...] + gate2 * _dot(act_ref[...], wd_ref[...])


def _block(x, y1, y2, mod, g, w1, w2, w_up, w_conv, w_down, tile):
    bn, ln, _ = x.shape
    d1, d2 = y1.shape[2], y2.shape[2]
    per_batch = mod.shape[0] > 1
    hb = tile // FFN_HALO
    nhb = ln // FFN_HALO
    wc = jnp.pad(w_conv, ((0, SUBLANES - w_conv.shape[0]), (0, 0)))
    const2 = lambda b, t: (0, 0)
    const3 = lambda b, t: (0, 0, 0)
    prev = lambda b, t: (b, jnp.maximum(t * hb - 1, 0), 0)
    main = lambda b, t: (b, t, 0)
    nxt = lambda b, t: (b, jnp.minimum((t + 1) * hb, nhb - 1), 0)

    def rows3(d):
        return [pl.BlockSpec((1, FFN_HALO, d), prev), pl.BlockSpec((1, tile, d), main),
                pl.BlockSpec((1, FFN_HALO, d), nxt)]

    return pl.pallas_call(
        functools.partial(_block_kernel, tile=tile),
        grid=(bn, ln // tile),
        in_specs=rows3(D_MODEL) + rows3(d1) + rows3(d2) + [
            pl.BlockSpec((1, 6, D_MODEL), (lambda b, t: (b, 0, 0)) if per_batch else const3),
            pl.BlockSpec((1, D_MODEL), const2),
            pl.BlockSpec((d1, D_MODEL), const2),
            pl.BlockSpec((d2, D_MODEL), const2),
            pl.BlockSpec((D_MODEL, 2 * D_FF), const2),
            pl.BlockSpec((SUBLANES, 2 * D_FF), const2),
            pl.BlockSpec((D_FF, D_MODEL), const2),
        ],
        out_specs=pl.BlockSpec((1, tile, D_MODEL), main),
        out_shape=jax.ShapeDtypeStruct((bn, ln, D_MODEL), F32),
        scratch_shapes=[pltpu.VMEM((tile + 2 * FFN_HALO, D_MODEL), BF16),
                        pltpu.VMEM((tile, D_MODEL), F32),
                        pltpu.VMEM((tile, D_FF), BF16)],
        compiler_params=_cparams("parallel", "parallel"),
        name="block",
    )(x, x, x, y1, y1, y1, y2, y2, y2, mod, g, w1, w2, w_up.astype(BF16), wc, w_down.astype(BF16))


def _head_norm(xb, ones_ref, gain):
    sq = xb * xb
    hi = sq.astype(BF16)
    lo = (sq - hi.astype(F32)).astype(BF16)
    ss = _dot(hi, ones_ref[...]) + _dot(lo, ones_ref[...])
    return xb * lax.rsqrt(ss * (1.0 / HEAD_DIM) + EPS) * gain


def _rope(y, cos, sin, first):
    partner = jnp.where(first, pltpu.roll(y, LANES - ROPE_PAIRS, 1), pltpu.roll(y, ROPE_PAIRS, 1))
    return y * cos + partner * sin


def _inproj_qk_kernel(x_ref, mod_ref, g_ref, w_ref, cos_ref, sin_ref, gq_ref, gk_ref, ones_ref,
                      q_ref, kt_ref, v_ref, u_ref):
    h = _norm_mod(x_ref[0], g_ref[...], mod_ref[0, 0:1, :], mod_ref[0, 1:2, :])
    p = _dot(h.astype(BF16), w_ref[...])
    lane = lax.broadcasted_iota(jnp.int32, (1, LANES), 1)
    first = (lane % (2 * ROPE_PAIRS)) < ROPE_PAIRS
    cos, sin = cos_ref[...], sin_ref[...]
    for c in range(D_Q // LANES):
        cols = slice(c * LANES, (c + 1) * LANES)
        y = _head_norm(p[:, cols], ones_ref, gq_ref[...])
        q_ref[0, :, cols] = (_rope(y, cos, sin, first) * Q_SCALE).astype(BF16)
    for c in range(D_KV // LANES):
        src = slice(D_Q + c * LANES, D_Q + (c + 1) * LANES)
        y = _head_norm(p[:, src], ones_ref, gk_ref[...])
        kt_ref[0, c * LANES:(c + 1) * LANES, :] = _rope(y, cos, sin, first).T.astype(BF16)
    v_ref[0] = p[:, D_Q + D_KV:D_Q + 2 * D_KV].astype(BF16)
    u_ref[0] = p[:, D_Q + 2 * D_KV:]


def _kvprep_ctx_kernel(p_ref, gk_ref, ones_ref, kt_ref, v_ref):
    for c in range(D_KV // LANES):
        cols = slice(c * LANES, (c + 1) * LANES)
        kt_ref[0, cols, :] = _head_norm(p_ref[0, :, cols], ones_ref, gk_ref[...]).T.astype(BF16)
    v_ref[0] = p_ref[0, :, D_KV:2 * D_KV].astype(BF16)


def _group_ones():
    r = np.arange(LANES) // HEAD_DIM
    return jnp.asarray(r[:, None] == r[None, :], BF16)


def _rope_tables(seq):
    t = np.arange(seq)
    freqs = ROPE_THETA ** (-np.arange(ROPE_PAIRS, dtype=np.float64) / ROPE_PAIRS)
    ang_r = (t // GRID_W).astype(np.float64)[:, None] * freqs
    ang_c = (t % GRID_W).astype(np.float64)[:, None] * freqs
    cos = np.concatenate([np.cos(ang_r)] * 2 + [np.cos(ang_c)] * 2, axis=1)
    sin = np.concatenate([-np.sin(ang_r), np.sin(ang_r), -np.sin(ang_c), np.sin(ang_c)], axis=1)
    reps = (1, LANES // HEAD_DIM)
    return jnp.asarray(np.tile(cos, reps), F32), jnp.asarray(np.tile(sin, reps), F32)


def _inproj_qk(x, mod, g, w, q_g, k_g, tile):
    bn, ln, _ = x.shape
    n = w.shape[1]
    cos, sin = _rope_tables(ln)
    gq = jnp.tile(q_g, LANES // HEAD_DIM).reshape(1, LANES)
    gk = jnp.tile(k_g, LANES // HEAD_DIM).reshape(1, LANES)
    row = lambda b, t: (b, t, 0)
    tab = lambda b, t: (t, 0)
    const = lambda b, t: (0, 0)
    return pl.pallas_call(
        _inproj_qk_kernel,
        grid=(bn, ln // tile),
        in_specs=[
            pl.BlockSpec((1, tile, D_MODEL), row),
            pl.BlockSpec((1, 6, D_MODEL), lambda b, t: (b, 0, 0)),
            pl.BlockSpec((1, D_MODEL), const),
            pl.BlockSpec((D_MODEL, n), const),
            pl.BlockSpec((tile, LANES), tab),
            pl.BlockSpec((tile, LANES), tab),
            pl.BlockSpec((1, LANES), const),
            pl.BlockSpec((1, LANES), const),
            pl.BlockSpec((LANES, LANES), const),
        ],
        out_specs=[pl.BlockSpec((1, tile, D_Q), row),
                   pl.BlockSpec((1, D_KV, tile), lambda b, t: (b, 0, t)),
                   pl.BlockSpec((1, tile, D_KV), row),
                   pl.BlockSpec((1, tile, D_POOL), row)],
        out_shape=[jax.ShapeDtypeStruct((bn, ln, D_Q), BF16),
                   jax.ShapeDtypeStruct((bn, D_KV, ln), BF16),
                   jax.ShapeDtypeStruct((bn, ln, D_KV), BF16),
                   jax.ShapeDtypeStruct((bn, ln, D_POOL), F32)],
        compiler_params=_cparams("parallel", "parallel"),
        name="inproj_qk",
    )(x, mod, g, w, cos, sin, gq, gk, _group_ones())


def _kvprep_ctx(pc, k_g):
    bn, lc, n = pc.shape
    gk = jnp.tile(k_g, LANES // HEAD_DIM).reshape(1, LANES)
    return pl.pallas_call(
        _kvprep_ctx_kernel,
        grid=(bn,),
        in_specs=[
            pl.BlockSpec((1, lc, n), lambda b: (b, 0, 0)),
            pl.BlockSpec((1, LANES), lambda b: (0, 0)),
            pl.BlockSpec((LANES, LANES), lambda b: (0, 0)),
        ],
        out_specs=[pl.BlockSpec((1, D_KV, lc), lambda b: (b, 0, 0)),
                   pl.BlockSpec((1, lc, D_KV), lambda b: (b, 0, 0))],
        out_shape=[jax.ShapeDtypeStruct((bn, D_KV, lc), BF16),
                   jax.ShapeDtypeStruct((bn, lc, D_KV), BF16)],
        compiler_params=_cparams("parallel"),
        name="kvprep_ctx",
    )(pc, gk, _group_ones())


def _attn_kernel(q_ref, kt_ref, v_ref, o_ref, vlo_ref, vhi_ref):
    low = lax.broadcasted_iota(jnp.int32, (1, LANES), 1) < HEAD_DIM

    @pl.when(pl.program_id(2) == 0)
    def _():
        v = v_ref[0]
        one = jnp.ones_like(v)
        vlo_ref[...] = jnp.where(low, v, one)
        vhi_ref[...] = jnp.where(low, one, v)

    kt = kt_ref[0]
    for r0 in range(0, q_ref.shape[1], ATTN_SUB):
        rows = slice(r0, r0 + ATTN_SUB)
        for c in range(q_ref.shape[2] // LANES):
            cols = slice(c * LANES, (c + 1) * LANES)
            qp = q_ref[0, rows, cols]
            zero = jnp.zeros_like(qp)
            halves = []
            for qm, vm_ref in ((jnp.where(low, qp, zero), vlo_ref), (jnp.where(low, zero, qp), vhi_ref)):
                s = _dot(qm, kt)
                e = jnp.exp2(s - jnp.max(s, axis=-1, keepdims=True))
                o = _dot(e.astype(BF16), vm_ref[...])
                halves.append(o / pltpu.roll(o, HEAD_DIM, 1))
            o_ref[0, rows, cols] = jnp.where(low, halves[0], halves[1]).astype(o_ref.dtype)


def _attention(q, kt_all, v_all, tile):
    bn, ln, _ = q.shape
    lk = v_all.shape[1]
    qw = D_Q // 2
    kw = D_KV // 2
    return pl.pallas_call(
        _attn_kernel,
        grid=(bn, 2, ln // tile),
        in_specs=[
            pl.BlockSpec((1, tile, qw), lambda b, j, t: (b, t, j)),
            pl.BlockSpec((1, kw, lk), lambda b, j, t: (b, j, 0)),
            pl.BlockSpec((1, lk, kw), lambda b, j, t: (b, 0, j)),
        ],
        out_specs=pl.BlockSpec((1, tile, qw), lambda b, j, t: (b, t, j)),
        out_shape=jax.ShapeDtypeStruct((bn, ln, D_Q), BF16),
        scratch_shapes=[pltpu.VMEM((lk, kw), BF16), pltpu.VMEM((lk, kw), BF16)],
        compiler_params=_cparams("parallel", "parallel", "arbitrary"),
        name="attention",
    )(q, kt_all, v_all)


def _pool_kernel(u_ref, cnt_ref, w_ref, sc_ref, o_ref, pad_ref, *, seq):
    n = seq + 2 * POOL_PAD
    zeros = jnp.zeros((POOL_PAD, D_POOL), F32)
    pad_ref[0:POOL_PAD, :] = zeros
    pad_ref[seq + POOL_PAD:n, :] = zeros
    pad_ref[POOL_PAD:seq + POOL_PAD, :] = u_ref[0]
    a = pad_ref[...]
    w2 = a + pltpu.roll(a, 1, 0)
    w4 = pltpu.roll(w2, 1, 0) + pltpu.roll(w2, n - 1, 0)
    w8 = pltpu.roll(w4, 2, 0) + pltpu.roll(w4, n - 2, 0)
    w16 = pltpu.roll(w8, 4, 0) + pltpu.roll(w8, n - 4, 0)
    lane = lax.broadcasted_iota(jnp.int32, (1, D_POOL), 1)
    g = lane // POOL_GROUP_DIM
    win = jnp.where(g == 0, w2, jnp.where(g == 1, w4, jnp.where(g == 2, w8, w16)))
    u = u_ref[0]
    pooled = win[POOL_PAD:seq + POOL_PAD, :] / cnt_ref[...] - u
    o_ref[0] = (_dot(pooled.astype(BF16), w_ref[...]) * sc_ref[...]).astype(o_ref.dtype)


def _pool(p, pool_w, pool_scale):
    bn, ln, n = p.shape
    t = np.arange(ln)
    cnt = jnp.asarray(np.concatenate(
        [np.broadcast_to((np.minimum(t + w // 2, ln) - np.maximum(t - w // 2, 0))[:, None], (ln, POOL_GROUP_DIM))
         for w in POOL_WINDOWS], axis=1), F32)
    wbd = jax.scipy.linalg.block_diag(*[pool_w[i] for i in range(pool_w.shape[0])]).astype(BF16)
    return pl.pallas_call(
        functools.partial(_pool_kernel, seq=ln),
        grid=(bn,),
        in_specs=[
            pl.BlockSpec((1, ln, D_POOL), lambda b: (b, 0, n // D_POOL - 1)),
            pl.BlockSpec((ln, D_POOL), lambda b: (0, 0)),
            pl.BlockSpec((D_POOL, D_POOL), lambda b: (0, 0)),
            pl.BlockSpec((1, D_POOL), lambda b: (0, 0)),
        ],
        out_specs=pl.BlockSpec((1, ln, D_POOL), lambda b: (b, 0, 0)),
        out_shape=jax.ShapeDtypeStruct((bn, ln, D_POOL), BF16),
        scratch_shapes=[pltpu.VMEM((ln + 2 * POOL_PAD, D_POOL), F32)],
        compiler_params=_cparams("parallel"),
        name="pool",
    )(p, cnt, wbd, pool_scale.reshape(1, -1))


def _paired_head_order():
    order = []
    per_kv = N_Q_HEADS // N_KV_HEADS
    for j in range(N_KV_HEADS // 2):
        for i in range(per_kv):
            order += [(2 * j) * per_kv + i, (2 * j + 1) * per_kv + i]
    return order


def _even_layer(x, mod, norm1_g, norm2_g, w_in, conv_w, ln_g, ln_b, w_out, w_up, w_conv, w_down, tile):
    a, b = _inproj_even(x, mod, norm1_g.reshape(1, -1), w_in.astype(BF16), tile)
    fa = _dft(a, min(512, a.shape[1]))
    bb = _convmod(b, conv_w, ln_g, ln_b)
    wo = w_out.astype(BF16)
    return _block(x, fa, bb, mod, norm2_g.reshape(1, -1), wo[:D_A], wo[D_A:], w_up, w_conv, w_down, tile)


def _odd_layer_last(x, ctx, mod, modc, norm1_g, norm2_g, w_in, q_g, k_g, pool_w, pool_scale, w_out,
                    w_up, w_conv, w_down, tile):
    heads = jnp.asarray(_paired_head_order())
    qcols = (heads[:, None] * HEAD_DIM + jnp.arange(HEAD_DIM)[None, :]).reshape(-1)
    w_in_b = w_in.astype(BF16)
    w_lat = jnp.concatenate([w_in_b[:, qcols], w_in_b[:, D_Q:]], axis=1)
    g1 = norm1_g.reshape(1, -1)
    q, kt, v, u = _inproj_qk(x, mod, g1, w_lat, q_g, k_g, tile)
    pc = _inproj(ctx, modc, g1, w_in_b[:, D_Q:D_Q + 2 * D_KV], ctx.shape[1])
    kct, vc = _kvprep_ctx(pc, k_g)
    attn = _attention(q, jnp.concatenate([kct, kt], axis=2), jnp.concatenate([vc, v], axis=1), ATTN_TILE)
    pooled = _pool(u, pool_w, pool_scale)
    wo = w_out.astype(BF16)
    return _block(x, attn, pooled, mod, norm2_g.reshape(1, -1), wo[:D_Q][qcols], wo[D_Q:],
                  w_up, w_conv, w_down, tile)


def kernel(x, c, ctx, c_ctx, w_ada, b_ada, norm1_g, norm2_g, ev_w_in, ev_conv_w, ev_ln_g, ev_ln_b, ev_w_out,
           od_w_in, od_q_g, od_k_g, od_pool_w, od_pool_scale, od_w_out, ffn_w_up, ffn_conv_w, ffn_w_down):
    depth = w_ada.shape[0]
    assert depth == 2, "even layer followed by a final odd layer"
    bn = x.shape[0]
    rows = -(-(bn + 1) // 8) * 8
    cc = jnp.concatenate([c, c_ctx[None, :], jnp.zeros((rows - bn - 1, D_MODEL), F32)], axis=0)
    mods = _ada(cc, w_ada, b_ada)
    mod = [mods[i, :bn].reshape(bn, 6, D_MODEL) for i in range(depth)]
    modc = [mods[i, bn:bn + 1].reshape(1, 6, D_MODEL) for i in range(depth)]

    tile = 512
    ev = (ev_w_in[0], ev_conv_w[0], ev_ln_g[0], ev_ln_b[0], ev_w_out[0],
          ffn_w_up[0], ffn_conv_w[0], ffn_w_down[0])
    x = _even_layer(x, mod[0], norm1_g[0], norm2_g[0], *ev, tile)
    ctx = _even_layer(ctx, modc[0], norm1_g[0], norm2_g[0], *ev, ctx.shape[1])
    return _odd_layer_last(x, ctx, mod[1], modc[1], norm1_g[1], norm2_g[1], od_w_in[0], od_q_g[0], od_k_g[0],
                           od_pool_w[0], od_pool_scale[0], od_w_out[0],
                           ffn_w_up[1], ffn_conv_w[1], ffn_w_down[1], tile)
```

```python
import functools

import jax
import jax.numpy as jnp
import numpy as np
from jax import lax
from jax.experimental import pallas as pl
from jax.experimental.pallas import tpu as pltpu

F32 = jnp.float32
BF16 = jnp.bfloat16

D_MODEL = 1024
GRID_W = 64
EPS = 1e-6

A_GROUPS = 4
A_GROUP_DIM = 128
D_A = A_GROUPS * A_GROUP_DIM
D_B = 512
CONV_WIDTH = 31
CONV_PAD = 16
CONV_BLOCK = 8

HEAD_DIM = 64
N_Q_HEADS = 12
N_KV_HEADS = 4
D_Q = N_Q_HEADS * HEAD_DIM
D_KV = N_KV_HEADS * HEAD_DIM
POOL_WINDOWS = (2, 4, 8, 16)
POOL_GROUP_DIM = 64
D_POOL = len(POOL_WINDOWS) * POOL_GROUP_DIM
POOL_PAD = 16
ROPE_THETA = 10000.0
ROPE_PAIRS = HEAD_DIM // 4
Q_SCALE = HEAD_DIM ** -0.5 * 1.4426950408889634

D_FF = 2816
FFN_CHUNK = 256
ATTN_TILE = 1024
ATTN_SUB = 512
FFN_HALO = 16

LANES = 128
SUBLANES = 8
VMEM_LIMIT = 56 * 1024 * 1024


def _cparams(*sem):
    return pltpu.CompilerParams(dimension_semantics=sem, vmem_limit_bytes=VMEM_LIMIT)


def _sigmoid(x):
    return 1.0 / (1.0 + jnp.exp(-x))


def _silu(x):
    return x * _sigmoid(x)


def _norm_mod(x, g, shift, scale):
    ms = jnp.mean(x * x, axis=-1, keepdims=True)
    y = x * lax.rsqrt(ms + EPS) * g
    return y * (1.0 + scale) + shift


def _dot(a, b):
    return jnp.dot(a, b, preferred_element_type=F32)


def _ada_kernel(cc_ref, w_ref, b_ref, o_ref):
    s = _silu(cc_ref[...])
    o_ref[0] = _dot(s.astype(BF16), w_ref[0].astype(BF16)) + b_ref[0]


def _ada(cc, w_ada, b_ada):
    depth = w_ada.shape[0]
    rows = cc.shape[0]
    nblk = w_ada.shape[2] // D_MODEL
    return pl.pallas_call(
        _ada_kernel,
        grid=(depth, nblk),
        in_specs=[
            pl.BlockSpec((rows, D_MODEL), lambda i, n: (0, 0)),
            pl.BlockSpec((1, D_MODEL, D_MODEL), lambda i, n: (i, 0, n)),
            pl.BlockSpec((1, 1, D_MODEL), lambda i, n: (i, 0, n)),
        ],
        out_specs=pl.BlockSpec((1, rows, D_MODEL), lambda i, n: (i, 0, n)),
        out_shape=jax.ShapeDtypeStruct((depth, rows, w_ada.shape[2]), F32),
        compiler_params=_cparams("parallel", "parallel"),
        name="ada",
    )(cc, w_ada, b_ada.reshape(depth, 1, -1))


def _inproj_kernel(x_ref, mod_ref, g_ref, w_ref, o_ref):
    h = _norm_mod(x_ref[0], g_ref[...], mod_ref[0, 0:1, :], mod_ref[0, 1:2, :])
    o_ref[0] = _dot(h.astype(BF16), w_ref[...])


def _inproj(x, mod, g, w, tile):
    bn, ln, _ = x.shape
    n = w.shape[1]
    per_batch = mod.shape[0] > 1
    return pl.pallas_call(
        _inproj_kernel,
        grid=(bn, ln // tile),
        in_specs=[
            pl.BlockSpec((1, tile, D_MODEL), lambda b, t: (b, t, 0)),
            pl.BlockSpec((1, 6, D_MODEL), (lambda b, t: (b, 0, 0)) if per_batch else (lambda b, t: (0, 0, 0))),
            pl.BlockSpec((1, D_MODEL), lambda b, t: (0, 0)),
            pl.BlockSpec((D_MODEL, n), lambda b, t: (0, 0)),
        ],
        out_specs=pl.BlockSpec((1, tile, n), lambda b, t: (b, t, 0)),
        out_shape=jax.ShapeDtypeStruct((bn, ln, n), F32),
        compiler_params=_cparams("parallel", "parallel"),
        name="inproj",
    )(x, mod, g, w)


def _inproj_even_kernel(x_ref, mod_ref, g_ref, w_ref, a_ref, b_ref, *s_refs):
    h = _norm_mod(x_ref[0], g_ref[...], mod_ref[0, 0:1, :], mod_ref[0, 1:2, :])
    p = _dot(h.astype(BF16), w_ref[...])
    half = x_ref.shape[1] // 2
    for c, s_ref in enumerate(s_refs):
        cols = slice(c * LANES, (c + 1) * LANES)
        s_ref[...] = p[:, cols]
        a_ref[0, 0:half, cols] = s_ref[pl.ds(0, half, stride=2), :].astype(BF16)
        a_ref[0, half:2 * half, cols] = s_ref[pl.ds(1, half, stride=2), :].astype(BF16)
    b_ref[0] = p[:, D_A:D_A + D_B] * _sigmoid(p[:, D_A + D_B:])


def _inproj_even(x, mod, g, w, tile):
    bn, ln, _ = x.shape
    n = w.shape[1]
    per_batch = mod.shape[0] > 1
    row = lambda b, t: (b, t, 0)
    return pl.pallas_call(
        _inproj_even_kernel,
        grid=(bn, ln // tile),
        in_specs=[
            pl.BlockSpec((1, tile, D_MODEL), row),
            pl.BlockSpec((1, 6, D_MODEL), (lambda b, t: (b, 0, 0)) if per_batch else (lambda b, t: (0, 0, 0))),
            pl.BlockSpec((1, D_MODEL), lambda b, t: (0, 0)),
            pl.BlockSpec((D_MODEL, n), lambda b, t: (0, 0)),
        ],
        out_specs=[pl.BlockSpec((1, tile, D_A), row), pl.BlockSpec((1, tile, D_B), row)],
        out_shape=[jax.ShapeDtypeStruct((bn, ln, D_A), BF16), jax.ShapeDtypeStruct((bn, ln, D_B), F32)],
        scratch_shapes=[pltpu.VMEM((tile, LANES), F32) for _ in range(D_A // LANES)],
        compiler_params=_cparams("parallel", "parallel"),
        name="inproj_even",
    )(x, mod, g, w)


def _dft_kernel(a_ref, cs_ref, me_ref, mo_ref, o_ref, ze_ref, zo_ref, *, seq, tile, scale):
    half = seq // 2

    @pl.when(pl.program_id(1) == 0)
    def _():
        for g in range(A_GROUPS):
            cols = slice(g * A_GROUP_DIM, (g + 1) * A_GROUP_DIM)
            xg = _dot(a_ref[0, :, cols], cs_ref[...]).astype(BF16)
            for t in range(seq // tile):
                for z_ref, src in ((ze_ref, t * tile), (zo_ref, t * tile + tile // 2)):
                    dst = t * (tile // 2)
                    z_ref[dst:dst + tile // 2, cols] = xg[src:src + tile // 2, :A_GROUP_DIM]
                    z_ref[half + dst:half + dst + tile // 2, cols] = xg[src:src + tile // 2, A_GROUP_DIM:]

    e = _dot(me_ref[...], ze_ref[...])
    o = _dot(mo_ref[...], zo_ref[...])
    o_ref[0, 0] = ((e + o) * scale).astype(o_ref.dtype)
    o_ref[0, 1] = ((e - o) * scale).astype(o_ref.dtype)


def _dft_matrices(seq):
    def cos_sin(n, cols):
        rows = np.arange(n // 2 if cols is not None else n, dtype=np.int64)
        cols = np.arange(n, dtype=np.int64) if cols is None else cols
        ang = ((rows[:, None] * cols[None, :]) % n).astype(np.float64) * (2.0 * np.pi / n)
        return np.cos(ang), np.sin(ang)
    cc, sc = cos_sin(A_GROUP_DIM, None)
    pos = np.arange(seq, dtype=np.int64)
    mats = []
    for parity in (0, 1):
        cl, sl = cos_sin(seq, pos[parity::2])
        mats.append(jnp.asarray(np.concatenate([cl, -sl], axis=1).astype(BF16)))
    return jnp.asarray(np.concatenate([cc, sc], axis=1).astype(BF16)), mats[0], mats[1]


def _dft(a, tile, freq_block):
    bn, ln, _ = a.shape
    half = ln // 2
    cs, me, mo = _dft_matrices(ln)
    scale = float(1.0 / (ln * A_GROUP_DIM) ** 0.5)
    out = pl.pallas_call(
        functools.partial(_dft_kernel, seq=ln, tile=tile, scale=scale),
        grid=(bn, half // freq_block),
        in_specs=[
            pl.BlockSpec((1, ln, D_A), lambda b, k: (b, 0, 0)),
            pl.BlockSpec((A_GROUP_DIM, 2 * A_GROUP_DIM), lambda b, k: (0, 0)),
            pl.BlockSpec((freq_block, ln), lambda b, k: (k, 0)),
            pl.BlockSpec((freq_block, ln), lambda b, k: (k, 0)),
        ],
        out_specs=pl.BlockSpec((1, 2, freq_block, D_A), lambda b, k: (b, 0, k, 0)),
        out_shape=jax.ShapeDtypeStruct((bn, 2, half, D_A), BF16),
        scratch_shapes=[pltpu.VMEM((ln, D_A), BF16), pltpu.VMEM((ln, D_A), BF16)],
        compiler_params=_cparams("parallel", "arbitrary"),
        name="dft",
    )(a, cs, me, mo)
    return out.reshape(bn, ln, D_A)


def _convmod_kernel(*refs, seq):
    ncb = D_B // LANES
    b_refs = refs[0:ncb]
    cw_ref, lg_ref, lb_ref, o_ref = refs[ncb:ncb + 4]
    e_refs, y_refs = refs[ncb + 4:2 * ncb + 4], refs[2 * ncb + 4:3 * ncb + 4]
    grp = seq // SUBLANES
    halo = CONV_PAD * SUBLANES
    sub = lax.broadcasted_iota(jnp.int32, (halo, 1), 0) % SUBLANES

    rows = CONV_BLOCK * SUBLANES
    norm_rows = 128

    for c in range(ncb):
        cols = slice(c * LANES, (c + 1) * LANES)

        def permute(i, carry, c=c):
            for j in range(SUBLANES):
                k = i * SUBLANES + j
                dst = pl.multiple_of(halo + k * SUBLANES, SUBLANES)
                e_refs[c][pl.ds(dst, SUBLANES), :] = b_refs[c][0, pl.ds(k, SUBLANES, stride=grp), :]
            return carry

        lax.fori_loop(0, grp // SUBLANES, permute, 0)
        first = e_refs[c][halo:2 * halo, :]
        lastb = e_refs[c][seq:seq + halo, :]
        e_refs[c][0:halo, :] = jnp.where(sub == 0, 0.0, pltpu.roll(lastb, 1, 0))
        e_refs[c][seq + halo:seq + 2 * halo, :] = jnp.where(sub == SUBLANES - 1, 0.0, pltpu.roll(first, halo - 1, 0))

        taps = [cw_ref[t * SUBLANES:(t + 1) * SUBLANES, cols] for t in range(CONV_WIDTH)]

        def conv(i, carry, c=c, taps=taps):
            r0 = pl.multiple_of(i * rows, rows)
            acc = [[None, None] for _ in range(CONV_BLOCK)]
            for idx in range(CONV_BLOCK + CONV_WIDTH - 1):
                src = r0 + (idx + CONV_PAD - CONV_WIDTH // 2) * SUBLANES
                xin = e_refs[c][pl.ds(src, SUBLANES), :]
                for a in range(CONV_BLOCK):
                    t = idx - a
                    if 0 <= t < CONV_WIDTH:
                        term = xin * taps[t]
                        acc[a][t % 2] = term if acc[a][t % 2] is None else acc[a][t % 2] + term
            for a in range(CONV_BLOCK):
                y_refs[c][pl.ds(r0 + a * SUBLANES, SUBLANES), :] = acc[a][0] + acc[a][1]
            return carry

        lax.fori_loop(0, grp // CONV_BLOCK, conv, 0)

    def norm(i, carry):
        r0 = pl.multiple_of(i * norm_rows, norm_rows)
        accs = [y_refs[c][pl.ds(r0, norm_rows), :] for c in range(ncb)]
        mu = jnp.sum(sum(accs), axis=-1, keepdims=True) * (1.0 / D_B)
        cens = [a - mu for a in accs]
        var = jnp.sum(sum(a * a for a in cens), axis=-1, keepdims=True) * (1.0 / D_B)
        inv = lax.rsqrt(var + EPS)
        for c in range(ncb):
            cols = slice(c * LANES, (c + 1) * LANES)
            y_refs[c][pl.ds(r0, norm_rows), :] = _silu(cens[c] * inv * lg_ref[:, cols] + lb_ref[:, cols])
        return carry

    lax.fori_loop(0, seq // norm_rows, norm, 0, unroll=2)

    gb = grp // SUBLANES
    blk = SUBLANES * SUBLANES

    def unpermute(r2, carry):
        src = pl.multiple_of(r2 * 2 * blk, 2 * blk)
        dst = pl.multiple_of(r2 * 2 * SUBLANES, 2 * SUBLANES)
        for q in range(SUBLANES):
            for c in range(ncb):
                two = [y_refs[c][pl.ds(src + h * blk + q, SUBLANES, stride=SUBLANES), :] for h in range(2)]
                o_ref[0, pl.ds(dst + q * grp, 2 * SUBLANES), c * LANES:(c + 1) * LANES] = (
                    jnp.concatenate(two, axis=0).astype(o_ref.dtype))
        return carry

    lax.fori_loop(0, gb // 2, unpermute, 0)


def _convmod(b, conv_w, ln_g, ln_b):
    bn, ln, _ = b.shape
    ncb = D_B // LANES
    cw = jnp.repeat(conv_w, SUBLANES, axis=0)
    col_spec = lambda cb: pl.BlockSpec((1, ln, LANES), lambda i: (i, 0, cb))
    return pl.pallas_call(
        functools.partial(_convmod_kernel, seq=ln),
        grid=(bn,),
        in_specs=[col_spec(c) for c in range(ncb)] + [
            pl.BlockSpec((CONV_WIDTH * SUBLANES, D_B), lambda b: (0, 0)),
            pl.BlockSpec((1, D_B), lambda b: (0, 0)),
            pl.BlockSpec((1, D_B), lambda b: (0, 0)),
        ],
        out_specs=pl.BlockSpec((1, ln, D_B), lambda b: (b, 0, 0)),
        out_shape=jax.ShapeDtypeStruct((bn, ln, D_B), BF16),
        scratch_shapes=[pltpu.VMEM((ln + 2 * CONV_PAD * SUBLANES, LANES), F32) for _ in range(ncb)]
                       + [pltpu.VMEM((ln, LANES), F32) for _ in range(ncb)],
        compiler_params=_cparams("parallel"),
        name="convmod",
    )(*([b] * ncb), cw, ln_g.reshape(1, -1), ln_b.reshape(1, -1))


def _block_kernel(xp_ref, x_ref, xn_ref, ap_ref, a_ref, an_ref, bp_ref, b_ref, bn_ref, mod_ref, g_ref,
                  w1_ref, w2_ref, wu_ref, cw_ref, wd_ref, o_ref, h_ref, x1_ref, act_ref, *, tile):
    t = pl.program_id(1)
    last = pl.num_programs(1) - 1
    g = g_ref[...]
    gate1 = mod_ref[0, 2:3, :]
    shift, scale, gate2 = mod_ref[0, 3:4, :], mod_ref[0, 4:5, :], mod_ref[0, 5:6, :]
    ext = tile + 2 * FFN_HALO

    def mixed(xr, ar, br):
        return xr[0] + gate1 * (_dot(ar[0], w1_ref[...]) + _dot(br[0], w2_ref[...]))

    x1_ref[...] = mixed(x_ref, a_ref, b_ref)
    hp = _norm_mod(mixed(xp_ref, ap_ref, bp_ref), g, shift, scale)
    hn = _norm_mod(mixed(xn_ref, an_ref, bn_ref), g, shift, scale)
    h_ref[0:FFN_HALO, :] = jnp.where(t > 0, hp, 0.0).astype(BF16)
    h_ref[FFN_HALO:FFN_HALO + tile, :] = _norm_mod(x1_ref[...], g, shift, scale).astype(BF16)
    h_ref[FFN_HALO + tile:ext, :] = jnp.where(t < last, hn, 0.0).astype(BF16)

    def conv3(up, cw):
        rows = slice(FFN_HALO, FFN_HALO + tile)
        prev = pltpu.roll(up, 1, 0)[rows, :]
        nxt = pltpu.roll(up, ext - 1, 0)[rows, :]
        return prev * cw[0:1, :] + up[rows, :] * cw[1:2, :] + nxt * cw[2:3, :]

    hx = h_ref[...]
    for j in range(D_FF // FFN_CHUNK):
        gcols = slice(j * FFN_CHUNK, (j + 1) * FFN_CHUNK)
        vcols = slice(D_FF + j * FFN_CHUNK, D_FF + (j + 1) * FFN_CHUNK)
        ug = conv3(_dot(hx, wu_ref[:, gcols]), cw_ref[:, gcols])
        uv = conv3(_dot(hx, wu_ref[:, vcols]), cw_ref[:, vcols])
        act_ref[:, gcols] = (_silu(ug) * uv).astype(BF16)
    o_ref[0] = x1_ref[...] + gate2 * _dot(act_ref[...], wd_ref[...])


def _block(x, y1, y2, mod, g, w1, w2, w_up, w_conv, w_down, tile):
    bn, ln, _ = x.shape
    d1, d2 = y1.shape[2], y2.shape[2]
    per_batch = mod.shape[0] > 1
    hb = tile // FFN_HALO
    nhb = ln // FFN_HALO
    wc = jnp.pad(w_conv, ((0, SUBLANES - w_conv.shape[0]), (0, 0)))
    const2 = lambda b, t: (0, 0)
    const3 = lambda b, t: (0, 0, 0)
    prev = lambda b, t: (b, jnp.maximum(t * hb - 1, 0), 0)
    main = lambda b, t: (b, t, 0)
    nxt = lambda b, t: (b, jnp.minimum((t + 1) * hb, nhb - 1), 0)

    def rows3(d):
        return [pl.BlockSpec((1, FFN_HALO, d), prev), pl.BlockSpec((1, tile, d), main),
                pl.BlockSpec((1, FFN_HALO, d), nxt)]

    return pl.pallas_call(
        functools.partial(_block_kernel, tile=tile),
        grid=(bn, ln // tile),
        in_specs=rows3(D_MODEL) + rows3(d1) + rows3(d2) + [
            pl.BlockSpec((1, 6, D_MODEL), (lambda b, t: (b, 0, 0)) if per_batch else const3),
            pl.BlockSpec((1, D_MODEL), const2),
            pl.BlockSpec((d1, D_MODEL), const2),
            pl.BlockSpec((d2, D_MODEL), const2),
            pl.BlockSpec((D_MODEL, 2 * D_FF), const2),
            pl.BlockSpec((SUBLANES, 2 * D_FF), const2),
            pl.BlockSpec((D_FF, D_MODEL), const2),
        ],
        out_specs=pl.BlockSpec((1, tile, D_MODEL), main),
        out_shape=jax.ShapeDtypeStruct((bn, ln, D_MODEL), F32),
        scratch_shapes=[pltpu.VMEM((tile + 2 * FFN_HALO, D_MODEL), BF16),
                        pltpu.VMEM((tile, D_MODEL), F32),
                        pltpu.VMEM((tile, D_FF), BF16)],
        compiler_params=_cparams("parallel", "parallel"),
        name="block",
    )(x, x, x, y1, y1, y1, y2, y2, y2, mod, g, w1, w2, w_up.astype(BF16), wc, w_down.astype(BF16))


def _head_norm(xb, ones_ref, gain):
    sq = xb * xb
    hi = sq.astype(BF16)
    lo = (sq - hi.astype(F32)).astype(BF16)
    ss = _dot(jnp.concatenate([hi, lo], axis=1), ones_ref[...])
    return xb * lax.rsqrt(ss * (1.0 / HEAD_DIM) + EPS) * gain


def _rope(y, cos, sin, first):
    partner = jnp.where(first, pltpu.roll(y, LANES - ROPE_PAIRS, 1), pltpu.roll(y, ROPE_PAIRS, 1))
    return y * cos + partner * sin


def _inproj_qk_kernel(x_ref, mod_ref, g_ref, w_ref, cos_ref, sin_ref, gq_ref, gk_ref, ones_ref,
                      q_ref, kt_ref, v_ref, u_ref):
    h = _norm_mod(x_ref[0], g_ref[...], mod_ref[0, 0:1, :], mod_ref[0, 1:2, :])
    p = _dot(h.astype(BF16), w_ref[...])
    lane = lax.broadcasted_iota(jnp.int32, (1, LANES), 1)
    first = (lane % (2 * ROPE_PAIRS)) < ROPE_PAIRS
    cos, sin = cos_ref[...], sin_ref[...]
    for c in range(D_Q // LANES):
        cols = slice(c * LANES, (c + 1) * LANES)
        y = _head_norm(p[:, cols], ones_ref, gq_ref[...])
        q_ref[0, :, cols] = (_rope(y, cos, sin, first) * Q_SCALE).astype(BF16)
    for c in range(D_KV // LANES):
        src = slice(D_Q + c * LANES, D_Q + (c + 1) * LANES)
        y = _head_norm(p[:, src], ones_ref, gk_ref[...])
        kt_ref[0, c * LANES:(c + 1) * LANES, :] = _rope(y, cos, sin, first).T.astype(BF16)
    v_ref[0] = p[:, D_Q + D_KV:D_Q + 2 * D_KV].astype(BF16)
    u_ref[0] = p[:, D_Q + 2 * D_KV:]


def _kvprep_ctx_kernel(p_ref, gk_ref, ones_ref, kt_ref, v_ref):
    for c in range(D_KV // LANES):
        cols = slice(c * LANES, (c + 1) * LANES)
        kt_ref[0, cols, :] = _head_norm(p_ref[0, :, cols], ones_ref, gk_ref[...]).T.astype(BF16)
    v_ref[0] = p_ref[0, :, D_KV:2 * D_KV].astype(BF16)


def _group_ones():
    r = np.arange(LANES) // HEAD_DIM
    ones = r[:, None] == r[None, :]
    return jnp.asarray(np.concatenate([ones, ones], axis=0), BF16)


def _rope_tables(seq):
    t = np.arange(seq)
    freqs = ROPE_THETA ** (-np.arange(ROPE_PAIRS, dtype=np.float64) / ROPE_PAIRS)
    ang_r = (t // GRID_W).astype(np.float64)[:, None] * freqs
    ang_c = (t % GRID_W).astype(np.float64)[:, None] * freqs
    cos = np.concatenate([np.cos(ang_r)] * 2 + [np.cos(ang_c)] * 2, axis=1)
    sin = np.concatenate([-np.sin(ang_r), np.sin(ang_r), -np.sin(ang_c), np.sin(ang_c)], axis=1)
    reps = (1, LANES // HEAD_DIM)
    return jnp.asarray(np.tile(cos, reps), F32), jnp.asarray(np.tile(sin, reps), F32)


def _inproj_qk(x, mod, g, w, q_g, k_g, tile):
    bn, ln, _ = x.shape
    n = w.shape[1]
    cos, sin = _rope_tables(ln)
    gq = jnp.tile(q_g, LANES // HEAD_DIM).reshape(1, LANES)
    gk = jnp.tile(k_g, LANES // HEAD_DIM).reshape(1, LANES)
    row = lambda b, t: (b, t, 0)
    tab = lambda b, t: (t, 0)
    const = lambda b, t: (0, 0)
    return pl.pallas_call(
        _inproj_qk_kernel,
        grid=(bn, ln // tile),
        in_specs=[
            pl.BlockSpec((1, tile, D_MODEL), row),
            pl.BlockSpec((1, 6, D_MODEL), lambda b, t: (b, 0, 0)),
            pl.BlockSpec((1, D_MODEL), const),
            pl.BlockSpec((D_MODEL, n), const),
            pl.BlockSpec((tile, LANES), tab),
            pl.BlockSpec((tile, LANES), tab),
            pl.BlockSpec((1, LANES), const),
            pl.BlockSpec((1, LANES), const),
            pl.BlockSpec((2 * LANES, LANES), const),
        ],
        out_specs=[pl.BlockSpec((1, tile, D_Q), row),
                   pl.BlockSpec((1, D_KV, tile), lambda b, t: (b, 0, t)),
                   pl.BlockSpec((1, tile, D_KV), row),
                   pl.BlockSpec((1, tile, D_POOL), row)],
        out_shape=[jax.ShapeDtypeStruct((bn, ln, D_Q), BF16),
                   jax.ShapeDtypeStruct((bn, D_KV, ln), BF16),
                   jax.ShapeDtypeStruct((bn, ln, D_KV), BF16),
                   jax.ShapeDtypeStruct((bn, ln, D_POOL), F32)],
        compiler_params=_cparams("parallel", "parallel"),
        name="inproj_qk",
    )(x, mod, g, w, cos, sin, gq, gk, _group_ones())


def _kvprep_ctx(pc, k_g):
    bn, lc, n = pc.shape
    gk = jnp.tile(k_g, LANES // HEAD_DIM).reshape(1, LANES)
    return pl.pallas_call(
        _kvprep_ctx_kernel,
        grid=(bn,),
        in_specs=[
            pl.BlockSpec((1, lc, n), lambda b: (b, 0, 0)),
            pl.BlockSpec((1, LANES), lambda b: (0, 0)),
            pl.BlockSpec((2 * LANES, LANES), lambda b: (0, 0)),
        ],
        out_specs=[pl.BlockSpec((1, D_KV, lc), lambda b: (b, 0, 0)),
                   pl.BlockSpec((1, lc, D_KV), lambda b: (b, 0, 0))],
        out_shape=[jax.ShapeDtypeStruct((bn, D_KV, lc), BF16),
                   jax.ShapeDtypeStruct((bn, lc, D_KV), BF16)],
        compiler_params=_cparams("parallel"),
        name="kvprep_ctx",
    )(pc, gk, _group_ones())


def _attn_kernel(q_ref, kt_ref, v_ref, o_ref, vlo_ref, vhi_ref):
    low = lax.broadcasted_iota(jnp.int32, (1, LANES), 1) < HEAD_DIM

    @pl.when(pl.program_id(2) == 0)
    def _():
        v = v_ref[0]
        one = jnp.ones_like(v)
        vlo_ref[...] = jnp.where(low, v, one)
        vhi_ref[...] = jnp.where(low, one, v)

    kt = kt_ref[0]
    for r0 in range(0, q_ref.shape[1], ATTN_SUB):
        rows = slice(r0, r0 + ATTN_SUB)
        for c in range(q_ref.shape[2] // LANES):
            cols = slice(c * LANES, (c + 1) * LANES)
            qp = q_ref[0, rows, cols]
            zero = jnp.zeros_like(qp)
            halves = []
            for qm, vm_ref in ((jnp.where(low, qp, zero), vlo_ref), (jnp.where(low, zero, qp), vhi_ref)):
                s = _dot(qm, kt)
                e = jnp.exp2(s - jnp.max(s, axis=-1, keepdims=True))
                o = _dot(e.astype(BF16), vm_ref[...])
                halves.append(o / pltpu.roll(o, HEAD_DIM, 1))
            o_ref[0, rows, cols] = jnp.where(low, halves[0], halves[1]).astype(o_ref.dtype)


def _attention(q, kt_all, v_all, tile):
    bn, ln, _ = q.shape
    lk = v_all.shape[1]
    qw = D_Q // 2
    kw = D_KV // 2
    return pl.pallas_call(
        _attn_kernel,
        grid=(bn, 2, ln // tile),
        in_specs=[
            pl.BlockSpec((1, tile, qw), lambda b, j, t: (b, t, j)),
            pl.BlockSpec((1, kw, lk), lambda b, j, t: (b, j, 0)),
            pl.BlockSpec((1, lk, kw), lambda b, j, t: (b, 0, j)),
        ],
        out_specs=pl.BlockSpec((1, tile, qw), lambda b, j, t: (b, t, j)),
        out_shape=jax.ShapeDtypeStruct((bn, ln, D_Q), BF16),
        scratch_shapes=[pltpu.VMEM((lk, kw), BF16), pltpu.VMEM((lk, kw), BF16)],
        compiler_params=_cparams("parallel", "parallel", "arbitrary"),
        name="attention",
    )(q, kt_all, v_all)


def _pool_kernel(u_ref, cnt_ref, w_ref, sc_ref, o_ref, pad_ref, *, seq):
    n = seq + 2 * POOL_PAD
    zeros = jnp.zeros((POOL_PAD, D_POOL), F32)
    pad_ref[0:POOL_PAD, :] = zeros
    pad_ref[seq + POOL_PAD:n, :] = zeros
    pad_ref[POOL_PAD:seq + POOL_PAD, :] = u_ref[0]
    a = pad_ref[...]
    w2 = a + pltpu.roll(a, 1, 0)
    w4 = pltpu.roll(w2, 1, 0) + pltpu.roll(w2, n - 1, 0)
    w8 = pltpu.roll(w4, 2, 0) + pltpu.roll(w4, n - 2, 0)
    w16 = pltpu.roll(w8, 4, 0) + pltpu.roll(w8, n - 4, 0)
    lane = lax.broadcasted_iota(jnp.int32, (1, D_POOL), 1)
    g = lane // POOL_GROUP_DIM
    win = jnp.where(g == 0, w2, jnp.where(g == 1, w4, jnp.where(g == 2, w8, w16)))
    u = u_ref[0]
    pooled = win[POOL_PAD:seq + POOL_PAD, :] / cnt_ref[...] - u
    o_ref[0] = (_dot(pooled.astype(BF16), w_ref[...]) * sc_ref[...]).astype(o_ref.dtype)


def _pool(p, pool_w, pool_scale):
    bn, ln, n = p.shape
    t = np.arange(ln)
    cnt = jnp.asarray(np.concatenate(
        [np.broadcast_to((np.minimum(t + w // 2, ln) - np.maximum(t - w // 2, 0))[:, None], (ln, POOL_GROUP_DIM))
         for w in POOL_WINDOWS], axis=1), F32)
    wbd = jax.scipy.linalg.block_diag(*[pool_w[i] for i in range(pool_w.shape[0])]).astype(BF16)
    return pl.pallas_call(
        functools.partial(_pool_kernel, seq=ln),
        grid=(bn,),
        in_specs=[
            pl.BlockSpec((1, ln, D_POOL), lambda b: (b, 0, n // D_POOL - 1)),
            pl.BlockSpec((ln, D_POOL), lambda b: (0, 0)),
            pl.BlockSpec((D_POOL, D_POOL), lambda b: (0, 0)),
            pl.BlockSpec((1, D_POOL), lambda b: (0, 0)),
        ],
        out_specs=pl.BlockSpec((1, ln, D_POOL), lambda b: (b, 0, 0)),
        out_shape=jax.ShapeDtypeStruct((bn, ln, D_POOL), BF16),
        scratch_shapes=[pltpu.VMEM((ln + 2 * POOL_PAD, D_POOL), F32)],
        compiler_params=_cparams("parallel"),
        name="pool",
    )(p, cnt, wbd, pool_scale.reshape(1, -1))


def _paired_head_order():
    order = []
    per_kv = N_Q_HEADS // N_KV_HEADS
    for j in range(N_KV_HEADS // 2):
        for i in range(per_kv):
            order += [(2 * j) * per_kv + i, (2 * j + 1) * per_kv + i]
    return order


def _even_layer(x, mod, norm1_g, norm2_g, w_in, conv_w, ln_g, ln_b, w_out, w_up, w_conv, w_down, tile):
    a, b = _inproj_even(x, mod, norm1_g.reshape(1, -1), w_in.astype(BF16), tile)
    fa = _dft(a, tile, min(512, a.shape[1] // 2))
    bb = _convmod(b, conv_w, ln_g, ln_b)
    wo = w_out.astype(BF16)
    return _block(x, fa, bb, mod, norm2_g.reshape(1, -1), wo[:D_A], wo[D_A:], w_up, w_conv, w_down, tile)


def _odd_layer_last(x, ctx, mod, modc, norm1_g, norm2_g, w_in, q_g, k_g, pool_w, pool_scale, w_out,
                    w_up, w_conv, w_down, tile):
    heads = jnp.asarray(_paired_head_order())
    qcols = (heads[:, None] * HEAD_DIM + jnp.arange(HEAD_DIM)[None, :]).reshape(-1)
    w_in_b = w_in.astype(BF16)
    w_lat = jnp.concatenate([w_in_b[:, qcols], w_in_b[:, D_Q:]], axis=1)
    g1 = norm1_g.reshape(1, -1)
    q, kt, v, u = _inproj_qk(x, mod, g1, w_lat, q_g, k_g, tile)
    pc = _inproj(ctx, modc, g1, w_in_b[:, D_Q:D_Q + 2 * D_KV], ctx.shape[1])
    kct, vc = _kvprep_ctx(pc, k_g)
    attn = _attention(q, jnp.concatenate([kct, kt], axis=2), jnp.concatenate([vc, v], axis=1), ATTN_TILE)
    pooled = _pool(u, pool_w, pool_scale)
    wo = w_out.astype(BF16)
    return _block(x, attn, pooled, mod, norm2_g.reshape(1, -1), wo[:D_Q][qcols], wo[D_Q:],
                  w_up, w_conv, w_down, tile)


def kernel(x, c, ctx, c_ctx, w_ada, b_ada, norm1_g, norm2_g, ev_w_in, ev_conv_w, ev_ln_g, ev_ln_b, ev_w_out,
           od_w_in, od_q_g, od_k_g, od_pool_w, od_pool_scale, od_w_out, ffn_w_up, ffn_conv_w, ffn_w_down):
    depth = w_ada.shape[0]
    assert depth == 2, "even layer followed by a final odd layer"
    bn = x.shape[0]
    rows = -(-(bn + 1) // 8) * 8
    cc = jnp.concatenate([c, c_ctx[None, :], jnp.zeros((rows - bn - 1, D_MODEL), F32)], axis=0)
    mods = _ada(cc, w_ada, b_ada)
    mod = [mods[i, :bn].reshape(bn, 6, D_MODEL) for i in range(depth)]
    modc = [mods[i, bn:bn + 1].reshape(1, 6, D_MODEL) for i in range(depth)]

    tile = 512
    ev = (ev_w_in[0], ev_conv_w[0], ev_ln_g[0], ev_ln_b[0], ev_w_out[0],
          ffn_w_up[0], ffn_conv_w[0], ffn_w_down[0])
    x = _even_layer(x, mod[0], norm1_g[0], norm2_g[0], *ev, tile)
    ctx = _even_layer(ctx, modc[0], norm1_g[0], norm2_g[0], *ev, ctx.shape[1])
    return _odd_layer_last(x, ctx, mod[1], modc[1], norm1_g[1], norm2_g[1], od_w_in[0], od_q_g[0], od_k_g[0],
                           od_pool_w[0], od_pool_scale[0], od_w_out[0],
                           ffn_w_up[1], ffn_conv_w[1], ffn_w_down[1], tile)
```

```python
import functools

import jax
import jax.numpy as jnp
import numpy as np
from jax import lax
from jax.experimental import pallas as pl
from jax.experimental.pallas import tpu as pltpu

F32 = jnp.float32
BF16 = jnp.bfloat16

D_MODEL = 1024
GRID_W = 64
EPS = 1e-6

A_GROUPS = 4
A_GROUP_DIM = 128
D_A = A_GROUPS * A_GROUP_DIM
D_B = 512
CONV_WIDTH = 31
CONV_PAD = 16
CONV_BLOCK = 8

HEAD_DIM = 64
N_Q_HEADS = 12
N_KV_HEADS = 4
D_Q = N_Q_HEADS * HEAD_DIM
D_KV = N_KV_HEADS * HEAD_DIM
POOL_WINDOWS = (2, 4, 8, 16)
POOL_GROUP_DIM = 64
D_POOL = len(POOL_WINDOWS) * POOL_GROUP_DIM
POOL_PAD = 16
ROPE_THETA = 10000.0
ROPE_PAIRS = HEAD_DIM // 4
Q_SCALE = HEAD_DIM ** -0.5 * 1.4426950408889634

D_FF = 2816
FFN_CHUNK = 256
ATTN_TILE = 2048
ATTN_SUB = 512
FFN_HALO = 16

LANES = 128
SUBLANES = 8
VMEM_LIMIT = 56 * 1024 * 1024


def _cparams(*sem):
    return pltpu.CompilerParams(dimension_semantics=sem, vmem_limit_bytes=VMEM_LIMIT)


def _sigmoid(x):
    return 1.0 / (1.0 + jnp.exp(-x))


def _silu(x):
    return x * _sigmoid(x)


def _norm_mod(x, g, shift, scale):
    ms = jnp.mean(x * x, axis=-1, keepdims=True)
    y = x * lax.rsqrt(ms + EPS) * g
    return y * (1.0 + scale) + shift


def _dot(a, b):
    return jnp.dot(a, b, preferred_element_type=F32)


def _ada_kernel(cc_ref, w_ref, b_ref, o_ref):
    s = _silu(cc_ref[...])
    o_ref[0] = _dot(s.astype(BF16), w_ref[0].astype(BF16)) + b_ref[0]


def _ada(cc, w_ada, b_ada):
    depth = w_ada.shape[0]
    rows = cc.shape[0]
    nblk = w_ada.shape[2] // D_MODEL
    return pl.pallas_call(
        _ada_kernel,
        grid=(depth, nblk),
        in_specs=[
            pl.BlockSpec((rows, D_MODEL), lambda i, n: (0, 0)),
            pl.BlockSpec((1, D_MODEL, D_MODEL), lambda i, n: (i, 0, n)),
            pl.BlockSpec((1, 1, D_MODEL), lambda i, n: (i, 0, n)),
        ],
        out_specs=pl.BlockSpec((1, rows, D_MODEL), lambda i, n: (i, 0, n)),
        out_shape=jax.ShapeDtypeStruct((depth, rows, w_ada.shape[2]), F32),
        compiler_params=_cparams("parallel", "parallel"),
        name="ada",
    )(cc, w_ada, b_ada.reshape(depth, 1, -1))


def _inproj_kernel(x_ref, mod_ref, g_ref, w_ref, o_ref):
    h = _norm_mod(x_ref[0], g_ref[...], mod_ref[0, 0:1, :], mod_ref[0, 1:2, :])
    o_ref[0] = _dot(h.astype(BF16), w_ref[...])


def _inproj(x, mod, g, w, tile):
    bn, ln, _ = x.shape
    n = w.shape[1]
    per_batch = mod.shape[0] > 1
    return pl.pallas_call(
        _inproj_kernel,
        grid=(bn, ln // tile),
        in_specs=[
            pl.BlockSpec((1, tile, D_MODEL), lambda b, t: (b, t, 0)),
            pl.BlockSpec((1, 6, D_MODEL), (lambda b, t: (b, 0, 0)) if per_batch else (lambda b, t: (0, 0, 0))),
            pl.BlockSpec((1, D_MODEL), lambda b, t: (0, 0)),
            pl.BlockSpec((D_MODEL, n), lambda b, t: (0, 0)),
        ],
        out_specs=pl.BlockSpec((1, tile, n), lambda b, t: (b, t, 0)),
        out_shape=jax.ShapeDtypeStruct((bn, ln, n), F32),
        compiler_params=_cparams("parallel", "parallel"),
        name="inproj",
    )(x, mod, g, w)


def _inproj_even_kernel(x_ref, mod_ref, g_ref, w_ref, a_ref, b_ref, *s_refs):
    h = _norm_mod(x_ref[0], g_ref[...], mod_ref[0, 0:1, :], mod_ref[0, 1:2, :])
    p = _dot(h.astype(BF16), w_ref[...])
    half = x_ref.shape[1] // 2
    for c, s_ref in enumerate(s_refs):
        cols = slice(c * LANES, (c + 1) * LANES)
        s_ref[...] = p[:, cols]
        a_ref[0, 0:half, cols] = s_ref[pl.ds(0, half, stride=2), :].astype(BF16)
        a_ref[0, half:2 * half, cols] = s_ref[pl.ds(1, half, stride=2), :].astype(BF16)
    b_ref[0] = p[:, D_A:D_A + D_B] * _sigmoid(p[:, D_A + D_B:])


def _inproj_even(x, mod, g, w, tile):
    bn, ln, _ = x.shape
    n = w.shape[1]
    per_batch = mod.shape[0] > 1
    row = lambda b, t: (b, t, 0)
    return pl.pallas_call(
        _inproj_even_kernel,
        grid=(bn, ln // tile),
        in_specs=[
            pl.BlockSpec((1, tile, D_MODEL), row),
            pl.BlockSpec((1, 6, D_MODEL), (lambda b, t: (b, 0, 0)) if per_batch else (lambda b, t: (0, 0, 0))),
            pl.BlockSpec((1, D_MODEL), lambda b, t: (0, 0)),
            pl.BlockSpec((D_MODEL, n), lambda b, t: (0, 0)),
        ],
        out_specs=[pl.BlockSpec((1, tile, D_A), row), pl.BlockSpec((1, tile, D_B), row)],
        out_shape=[jax.ShapeDtypeStruct((bn, ln, D_A), BF16), jax.ShapeDtypeStruct((bn, ln, D_B), F32)],
        scratch_shapes=[pltpu.VMEM((tile, LANES), F32) for _ in range(D_A // LANES)],
        compiler_params=_cparams("parallel", "parallel"),
        name="inproj_even",
    )(x, mod, g, w)


def _dft_kernel(a_ref, cs_ref, me_ref, mo_ref, o_ref, ze_ref, zo_ref, *, seq, tile, scale):
    half = seq // 2

    @pl.when(pl.program_id(1) == 0)
    def _():
        for g in range(A_GROUPS):
            cols = slice(g * A_GROUP_DIM, (g + 1) * A_GROUP_DIM)
            xg = _dot(a_ref[0, :, cols], cs_ref[...]).astype(BF16)
            for t in range(seq // tile):
                for z_ref, src in ((ze_ref, t * tile), (zo_ref, t * tile + tile // 2)):
                    dst = t * (tile // 2)
                    z_ref[dst:dst + tile // 2, cols] = xg[src:src + tile // 2, :A_GROUP_DIM]
                    z_ref[half + dst:half + dst + tile // 2, cols] = xg[src:src + tile // 2, A_GROUP_DIM:]

    e = _dot(me_ref[...], ze_ref[...])
    o = _dot(mo_ref[...], zo_ref[...])
    o_ref[0, 0] = ((e + o) * scale).astype(o_ref.dtype)
    o_ref[0, 1] = ((e - o) * scale).astype(o_ref.dtype)


def _dft_matrices(seq):
    def cos_sin(n, cols):
        rows = np.arange(n // 2 if cols is not None else n, dtype=np.int64)
        cols = np.arange(n, dtype=np.int64) if cols is None else cols
        ang = ((rows[:, None] * cols[None, :]) % n).astype(np.float64) * (2.0 * np.pi / n)
        return np.cos(ang), np.sin(ang)
    cc, sc = cos_sin(A_GROUP_DIM, None)
    pos = np.arange(seq, dtype=np.int64)
    mats = []
    for parity in (0, 1):
        cl, sl = cos_sin(seq, pos[parity::2])
        mats.append(jnp.asarray(np.concatenate([cl, -sl], axis=1).astype(BF16)))
    return jnp.asarray(np.concatenate([cc, sc], axis=1).astype(BF16)), mats[0], mats[1]


def _dft(a, tile, freq_block):
    bn, ln, _ = a.shape
    half = ln // 2
    cs, me, mo = _dft_matrices(ln)
    scale = float(1.0 / (ln * A_GROUP_DIM) ** 0.5)
    out = pl.pallas_call(
        functools.partial(_dft_kernel, seq=ln, tile=tile, scale=scale),
        grid=(bn, half // freq_block),
        in_specs=[
            pl.BlockSpec((1, ln, D_A), lambda b, k: (b, 0, 0)),
            pl.BlockSpec((A_GROUP_DIM, 2 * A_GROUP_DIM), lambda b, k: (0, 0)),
            pl.BlockSpec((freq_block, ln), lambda b, k: (k, 0)),
            pl.BlockSpec((freq_block, ln), lambda b, k: (k, 0)),
        ],
        out_specs=pl.BlockSpec((1, 2, freq_block, D_A), lambda b, k: (b, 0, k, 0)),
        out_shape=jax.ShapeDtypeStruct((bn, 2, half, D_A), BF16),
        scratch_shapes=[pltpu.VMEM((ln, D_A), BF16), pltpu.VMEM((ln, D_A), BF16)],
        compiler_params=_cparams("parallel", "arbitrary"),
        name="dft",
    )(a, cs, me, mo)
    return out.reshape(bn, ln, D_A)


def _convmod_kernel(*refs, seq):
    ncb = D_B // LANES
    b_refs = refs[0:ncb]
    cw_ref, lg_ref, lb_ref, o_ref = refs[ncb:ncb + 4]
    e_refs, y_refs = refs[ncb + 4:2 * ncb + 4], refs[2 * ncb + 4:3 * ncb + 4]
    grp = seq // SUBLANES
    halo = CONV_PAD * SUBLANES
    sub = lax.broadcasted_iota(jnp.int32, (halo, 1), 0) % SUBLANES

    rows = CONV_BLOCK * SUBLANES
    norm_rows = 128

    for c in range(ncb):
        cols = slice(c * LANES, (c + 1) * LANES)

        def permute(i, carry, c=c):
            for j in range(SUBLANES):
                k = i * SUBLANES + j
                dst = pl.multiple_of(halo + k * SUBLANES, SUBLANES)
                e_refs[c][pl.ds(dst, SUBLANES), :] = b_refs[c][0, pl.ds(k, SUBLANES, stride=grp), :]
            return carry

        lax.fori_loop(0, grp // SUBLANES, permute, 0)
        first = e_refs[c][halo:2 * halo, :]
        lastb = e_refs[c][seq:seq + halo, :]
        e_refs[c][0:halo, :] = jnp.where(sub == 0, 0.0, pltpu.roll(lastb, 1, 0))
        e_refs[c][seq + halo:seq + 2 * halo, :] = jnp.where(sub == SUBLANES - 1, 0.0, pltpu.roll(first, halo - 1, 0))

        taps = [cw_ref[t * SUBLANES:(t + 1) * SUBLANES, cols] for t in range(CONV_WIDTH)]

        def conv(i, carry, c=c, taps=taps):
            r0 = pl.multiple_of(i * rows, rows)
            acc = [[None, None] for _ in range(CONV_BLOCK)]
            for idx in range(CONV_BLOCK + CONV_WIDTH - 1):
                src = r0 + (idx + CONV_PAD - CONV_WIDTH // 2) * SUBLANES
                xin = e_refs[c][pl.ds(src, SUBLANES), :]
                for a in range(CONV_BLOCK):
                    t = idx - a
                    if 0 <= t < CONV_WIDTH:
                        term = xin * taps[t]
                        acc[a][t % 2] = term if acc[a][t % 2] is None else acc[a][t % 2] + term
            for a in range(CONV_BLOCK):
                y_refs[c][pl.ds(r0 + a * SUBLANES, SUBLANES), :] = acc[a][0] + acc[a][1]
            return carry

        lax.fori_loop(0, grp // CONV_BLOCK, conv, 0)

    def norm(i, carry):
        r0 = pl.multiple_of(i * norm_rows, norm_rows)
        accs = [y_refs[c][pl.ds(r0, norm_rows), :] for c in range(ncb)]
        mu = jnp.sum(sum(accs), axis=-1, keepdims=True) * (1.0 / D_B)
        cens = [a - mu for a in accs]
        var = jnp.sum(sum(a * a for a in cens), axis=-1, keepdims=True) * (1.0 / D_B)
        inv = lax.rsqrt(var + EPS)
        for c in range(ncb):
            cols = slice(c * LANES, (c + 1) * LANES)
            y_refs[c][pl.ds(r0, norm_rows), :] = _silu(cens[c] * inv * lg_ref[:, cols] + lb_ref[:, cols])
        return carry

    lax.fori_loop(0, seq // norm_rows, norm, 0, unroll=2)

    gb = grp // SUBLANES
    blk = SUBLANES * SUBLANES

    def unpermute(r2, carry):
        src = pl.multiple_of(r2 * 2 * blk, 2 * blk)
        dst = pl.multiple_of(r2 * 2 * SUBLANES, 2 * SUBLANES)
        for q in range(SUBLANES):
            for c in range(ncb):
                two = [y_refs[c][pl.ds(src + h * blk + q, SUBLANES, stride=SUBLANES), :] for h in range(2)]
                o_ref[0, pl.ds(dst + q * grp, 2 * SUBLANES), c * LANES:(c + 1) * LANES] = (
                    jnp.concatenate(two, axis=0).astype(o_ref.dtype))
        return carry

    lax.fori_loop(0, gb // 2, unpermute, 0)


def _convmod(b, conv_w, ln_g, ln_b):
    bn, ln, _ = b.shape
    ncb = D_B // LANES
    cw = jnp.repeat(conv_w, SUBLANES, axis=0)
    col_spec = lambda cb: pl.BlockSpec((1, ln, LANES), lambda i: (i, 0, cb))
    return pl.pallas_call(
        functools.partial(_convmod_kernel, seq=ln),
        grid=(bn,),
        in_specs=[col_spec(c) for c in range(ncb)] + [
            pl.BlockSpec((CONV_WIDTH * SUBLANES, D_B), lambda b: (0, 0)),
            pl.BlockSpec((1, D_B), lambda b: (0, 0)),
            pl.BlockSpec((1, D_B), lambda b: (0, 0)),
        ],
        out_specs=pl.BlockSpec((1, ln, D_B), lambda b: (b, 0, 0)),
        out_shape=jax.ShapeDtypeStruct((bn, ln, D_B), BF16),
        scratch_shapes=[pltpu.VMEM((ln + 2 * CONV_PAD * SUBLANES, LANES), F32) for _ in range(ncb)]
                       + [pltpu.VMEM((ln, LANES), F32) for _ in range(ncb)],
        compiler_params=_cparams("parallel"),
        name="convmod",
    )(*([b] * ncb), cw, ln_g.reshape(1, -1), ln_b.reshape(1, -1))


def _block_kernel(xp_ref, x_ref, xn_ref, ap_ref, a_ref, an_ref, bp_ref, b_ref, bn_ref, mod_ref, g_ref,
                  w1_ref, w2_ref, wu_ref, cw_ref, wd_ref, o_ref, h_ref, x1_ref, act_ref, *, tile):
    t = pl.program_id(1)
    last = pl.num_programs(1) - 1
    g = g_ref[...]
    gate1 = mod_ref[0, 2:3, :]
    shift, scale, gate2 = mod_ref[0, 3:4, :], mod_ref[0, 4:5, :], mod_ref[0, 5:6, :]
    ext = tile + 2 * FFN_HALO

    def mixed(xr, ar, br):
        return xr[0] + gate1 * (_dot(ar[0], w1_ref[...]) + _dot(br[0], w2_ref[...]))

    x1_ref[...] = mixed(x_ref, a_ref, b_ref)
    hp = _norm_mod(mixed(xp_ref, ap_ref, bp_ref), g, shift, scale)
    hn = _norm_mod(mixed(xn_ref, an_ref, bn_ref), g, shift, scale)
    h_ref[0:FFN_HALO, :] = jnp.where(t > 0, hp, 0.0).astype(BF16)
    h_ref[FFN_HALO:FFN_HALO + tile, :] = _norm_mod(x1_ref[...], g, shift, scale).astype(BF16)
    h_ref[FFN_HALO + tile:ext, :] = jnp.where(t < last, hn, 0.0).astype(BF16)

    def conv3(up, cw):
        rows = slice(FFN_HALO, FFN_HALO + tile)
        prev = pltpu.roll(up, 1, 0)[rows, :]
        nxt = pltpu.roll(up, ext - 1, 0)[rows, :]
        return prev * cw[0:1, :] + up[rows, :] * cw[1:2, :] + nxt * cw[2:3, :]

    hx = h_ref[...]
    for j in range(D_FF // FFN_CHUNK):
        gcols = slice(j * FFN_CHUNK, (j + 1) * FFN_CHUNK)
        vcols = slice(D_FF + j * FFN_CHUNK, D_FF + (j + 1) * FFN_CHUNK)
        ug = conv3(_dot(hx, wu_ref[:, gcols]), cw_ref[:, gcols])
        uv = conv3(_dot(hx, wu_ref[:, vcols]), cw_ref[:, vcols])
        act_ref[:, gcols] = (_silu(ug) * uv).astype(BF16)
    o_ref[0] = x1_ref[...] + gate2 * _dot(act_ref[...], wd_ref[...])


def _block(x, y1, y2, mod, g, w1, w2, w_up, w_conv, w_down, tile):
    bn, ln, _ = x.shape
    d1, d2 = y1.shape[2], y2.shape[2]
    per_batch = mod.shape[0] > 1
    hb = tile // FFN_HALO
    nhb = ln // FFN_HALO
    wc = jnp.pad(w_conv, ((0, SUBLANES - w_conv.shape[0]), (0, 0)))
    const2 = lambda b, t: (0, 0)
    const3 = lambda b, t: (0, 0, 0)
    prev = lambda b, t: (b, jnp.maximum(t * hb - 1, 0), 0)
    main = lambda b, t: (b, t, 0)
    nxt = lambda b, t: (b, jnp.minimum((t + 1) * hb, nhb - 1), 0)

    def rows3(d):
        return [pl.BlockSpec((1, FFN_HALO, d), prev), pl.BlockSpec((1, tile, d), main),
                pl.BlockSpec((1, FFN_HALO, d), nxt)]

    return pl.pallas_call(
        functools.partial(_block_kernel, tile=tile),
        grid=(bn, ln // tile),
        in_specs=rows3(D_MODEL) + rows3(d1) + rows3(d2) + [
            pl.BlockSpec((1, 6, D_MODEL), (lambda b, t: (b, 0, 0)) if per_batch else const3),
            pl.BlockSpec((1, D_MODEL), const2),
            pl.BlockSpec((d1, D_MODEL), const2),
            pl.BlockSpec((d2, D_MODEL), const2),
            pl.BlockSpec((D_MODEL, 2 * D_FF), const2),
            pl.BlockSpec((SUBLANES, 2 * D_FF), const2),
            pl.BlockSpec((D_FF, D_MODEL), const2),
        ],
        out_specs=pl.BlockSpec((1, tile, D_MODEL), main),
        out_shape=jax.ShapeDtypeStruct((bn, ln, D_MODEL), F32),
        scratch_shapes=[pltpu.VMEM((tile + 2 * FFN_HALO, D_MODEL), BF16),
                        pltpu.VMEM((tile, D_MODEL), F32),
                        pltpu.VMEM((tile, D_FF), BF16)],
        compiler_params=_cparams("parallel", "parallel"),
        name="block",
    )(x, x, x, y1, y1, y1, y2, y2, y2, mod, g, w1, w2, w_up.astype(BF16), wc, w_down.astype(BF16))


def _head_norm(xb, ones_ref, gain):
    sq = xb * xb
    hi = sq.astype(BF16)
    lo = (sq - hi.astype(F32)).astype(BF16)
    ss = _dot(jnp.concatenate([hi, lo], axis=1), ones_ref[...])
    return xb * lax.rsqrt(ss * (1.0 / HEAD_DIM) + EPS) * gain


def _rope(y, cos, sin, first):
    partner = jnp.where(first, pltpu.roll(y, LANES - ROPE_PAIRS, 1), pltpu.roll(y, ROPE_PAIRS, 1))
    return y * cos + partner * sin


def _inproj_qk_kernel(x_ref, mod_ref, g_ref, w_ref, cos_ref, sin_ref, gq_ref, gk_ref, ones_ref,
                      q_ref, kt_ref, v_ref, u_ref):
    h = _norm_mod(x_ref[0], g_ref[...], mod_ref[0, 0:1, :], mod_ref[0, 1:2, :])
    p = _dot(h.astype(BF16), w_ref[...])
    lane = lax.broadcasted_iota(jnp.int32, (1, LANES), 1)
    first = (lane % (2 * ROPE_PAIRS)) < ROPE_PAIRS
    cos, sin = cos_ref[...], sin_ref[...]
    for c in range(D_Q // LANES):
        cols = slice(c * LANES, (c + 1) * LANES)
        y = _head_norm(p[:, cols], ones_ref, gq_ref[...])
        q_ref[0, :, cols] = (_rope(y, cos, sin, first) * Q_SCALE).astype(BF16)
    for c in range(D_KV // LANES):
        src = slice(D_Q + c * LANES, D_Q + (c + 1) * LANES)
        y = _head_norm(p[:, src], ones_ref, gk_ref[...])
        kt_ref[0, c * LANES:(c + 1) * LANES, :] = _rope(y, cos, sin, first).T.astype(BF16)
    v_ref[0] = p[:, D_Q + D_KV:D_Q + 2 * D_KV].astype(BF16)
    u_ref[0] = p[:, D_Q + 2 * D_KV:]


def _kvprep_ctx_kernel(p_ref, gk_ref, ones_ref, kt_ref, v_ref):
    for c in range(D_KV // LANES):
        cols = slice(c * LANES, (c + 1) * LANES)
        kt_ref[0, cols, :] = _head_norm(p_ref[0, :, cols], ones_ref, gk_ref[...]).T.astype(BF16)
    v_ref[0] = p_ref[0, :, D_KV:2 * D_KV].astype(BF16)


def _group_ones():
    r = np.arange(LANES) // HEAD_DIM
    ones = r[:, None] == r[None, :]
    return jnp.asarray(np.concatenate([ones, ones], axis=0), BF16)


def _rope_tables(seq):
    t = np.arange(seq)
    freqs = ROPE_THETA ** (-np.arange(ROPE_PAIRS, dtype=np.float64) / ROPE_PAIRS)
    ang_r = (t // GRID_W).astype(np.float64)[:, None] * freqs
    ang_c = (t % GRID_W).astype(np.float64)[:, None] * freqs
    cos = np.concatenate([np.cos(ang_r)] * 2 + [np.cos(ang_c)] * 2, axis=1)
    sin = np.concatenate([-np.sin(ang_r), np.sin(ang_r), -np.sin(ang_c), np.sin(ang_c)], axis=1)
    reps = (1, LANES // HEAD_DIM)
    return jnp.asarray(np.tile(cos, reps), F32), jnp.asarray(np.tile(sin, reps), F32)


def _inproj_qk(x, mod, g, w, q_g, k_g, tile):
    bn, ln, _ = x.shape
    n = w.shape[1]
    cos, sin = _rope_tables(ln)
    gq = jnp.tile(q_g, LANES // HEAD_DIM).reshape(1, LANES)
    gk = jnp.tile(k_g, LANES // HEAD_DIM).reshape(1, LANES)
    row = lambda b, t: (b, t, 0)
    tab = lambda b, t: (t, 0)
    const = lambda b, t: (0, 0)
    return pl.pallas_call(
        _inproj_qk_kernel,
        grid=(bn, ln // tile),
        in_specs=[
            pl.BlockSpec((1, tile, D_MODEL), row),
            pl.BlockSpec((1, 6, D_MODEL), lambda b, t: (b, 0, 0)),
            pl.BlockSpec((1, D_MODEL), const),
            pl.BlockSpec((D_MODEL, n), const),
            pl.BlockSpec((tile, LANES), tab),
            pl.BlockSpec((tile, LANES), tab),
            pl.BlockSpec((1, LANES), const),
            pl.BlockSpec((1, LANES), const),
            pl.BlockSpec((2 * LANES, LANES), const),
        ],
        out_specs=[pl.BlockSpec((1, tile, D_Q), row),
                   pl.BlockSpec((1, D_KV, tile), lambda b, t: (b, 0, t)),
                   pl.BlockSpec((1, tile, D_KV), row),
                   pl.BlockSpec((1, tile, D_POOL), row)],
        out_shape=[jax.ShapeDtypeStruct((bn, ln, D_Q), BF16),
                   jax.ShapeDtypeStruct((bn, D_KV, ln), BF16),
                   jax.ShapeDtypeStruct((bn, ln, D_KV), BF16),
                   jax.ShapeDtypeStruct((bn, ln, D_POOL), F32)],
        compiler_params=_cparams("parallel", "parallel"),
        name="inproj_qk",
    )(x, mod, g, w, cos, sin, gq, gk, _group_ones())


def _kvprep_ctx(pc, k_g):
    bn, lc, n = pc.shape
    gk = jnp.tile(k_g, LANES // HEAD_DIM).reshape(1, LANES)
    return pl.pallas_call(
        _kvprep_ctx_kernel,
        grid=(bn,),
        in_specs=[
            pl.BlockSpec((1, lc, n), lambda b: (b, 0, 0)),
            pl.BlockSpec((1, LANES), lambda b: (0, 0)),
            pl.BlockSpec((2 * LANES, LANES), lambda b: (0, 0)),
        ],
        out_specs=[pl.BlockSpec((1, D_KV, lc), lambda b: (b, 0, 0)),
                   pl.BlockSpec((1, lc, D_KV), lambda b: (b, 0, 0))],
        out_shape=[jax.ShapeDtypeStruct((bn, D_KV, lc), BF16),
                   jax.ShapeDtypeStruct((bn, lc, D_KV), BF16)],
        compiler_params=_cparams("parallel"),
        name="kvprep_ctx",
    )(pc, gk, _group_ones())


def _attn_kernel(q_ref, kt_ref, v_ref, o_ref, vlo_ref, vhi_ref):
    low = lax.broadcasted_iota(jnp.int32, (1, LANES), 1) < HEAD_DIM

    @pl.when(pl.program_id(2) == 0)
    def _():
        v = v_ref[0]
        one = jnp.ones_like(v)
        vlo_ref[...] = jnp.where(low, v, one)
        vhi_ref[...] = jnp.where(low, one, v)

    kt = kt_ref[0]
    for r0 in range(0, q_ref.shape[1], ATTN_SUB):
        rows = slice(r0, r0 + ATTN_SUB)
        for c in range(q_ref.shape[2] // LANES):
            cols = slice(c * LANES, (c + 1) * LANES)
            qp = q_ref[0, rows, cols]
            zero = jnp.zeros_like(qp)
            halves = []
            for qm, vm_ref in ((jnp.where(low, qp, zero), vlo_ref), (jnp.where(low, zero, qp), vhi_ref)):
                s = _dot(qm, kt)
                e = jnp.exp2(s - jnp.max(s, axis=-1, keepdims=True))
                o = _dot(e.astype(BF16), vm_ref[...])
                halves.append(o / pltpu.roll(o, HEAD_DIM, 1))
            o_ref[0, rows, cols] = jnp.where(low, halves[0], halves[1]).astype(o_ref.dtype)


def _attention(q, kt_all, v_all, tile):
    bn, ln, _ = q.shape
    lk = v_all.shape[1]
    qw = D_Q // 2
    kw = D_KV // 2
    return pl.pallas_call(
        _attn_kernel,
        grid=(bn, 2, ln // tile),
        in_specs=[
            pl.BlockSpec((1, tile, qw), lambda b, j, t: (b, t, j)),
            pl.BlockSpec((1, kw, lk), lambda b, j, t: (b, j, 0)),
            pl.BlockSpec((1, lk, kw), lambda b, j, t: (b, 0, j)),
        ],
        out_specs=pl.BlockSpec((1, tile, qw), lambda b, j, t: (b, t, j)),
        out_shape=jax.ShapeDtypeStruct((bn, ln, D_Q), BF16),
        scratch_shapes=[pltpu.VMEM((lk, kw), BF16), pltpu.VMEM((lk, kw), BF16)],
        compiler_params=_cparams("parallel", "parallel", "arbitrary"),
        name="attention",
    )(q, kt_all, v_all)


def _pool_kernel(u_ref, cnt_ref, w_ref, sc_ref, o_ref, pad_ref, *, seq):
    n = seq + 2 * POOL_PAD
    zeros = jnp.zeros((POOL_PAD, D_POOL), F32)
    pad_ref[0:POOL_PAD, :] = zeros
    pad_ref[seq + POOL_PAD:n, :] = zeros
    pad_ref[POOL_PAD:seq + POOL_PAD, :] = u_ref[0]
    a = pad_ref[...]
    w2 = a + pltpu.roll(a, 1, 0)
    w4 = pltpu.roll(w2, 1, 0) + pltpu.roll(w2, n - 1, 0)
    w8 = pltpu.roll(w4, 2, 0) + pltpu.roll(w4, n - 2, 0)
    w16 = pltpu.roll(w8, 4, 0) + pltpu.roll(w8, n - 4, 0)
    lane = lax.broadcasted_iota(jnp.int32, (1, D_POOL), 1)
    g = lane // POOL_GROUP_DIM
    win = jnp.where(g == 0, w2, jnp.where(g == 1, w4, jnp.where(g == 2, w8, w16)))
    u = u_ref[0]
    pooled = win[POOL_PAD:seq + POOL_PAD, :] / cnt_ref[...] - u
    o_ref[0] = (_dot(pooled.astype(BF16), w_ref[...]) * sc_ref[...]).astype(o_ref.dtype)


def _pool(p, pool_w, pool_scale):
    bn, ln, n = p.shape
    t = np.arange(ln)
    cnt = jnp.asarray(np.concatenate(
        [np.broadcast_to((np.minimum(t + w // 2, ln) - np.maximum(t - w // 2, 0))[:, None], (ln, POOL_GROUP_DIM))
         for w in POOL_WINDOWS], axis=1), F32)
    wbd = jax.scipy.linalg.block_diag(*[pool_w[i] for i in range(pool_w.shape[0])]).astype(BF16)
    return pl.pallas_call(
        functools.partial(_pool_kernel, seq=ln),
        grid=(bn,),
        in_specs=[
            pl.BlockSpec((1, ln, D_POOL), lambda b: (b, 0, n // D_POOL - 1)),
            pl.BlockSpec((ln, D_POOL), lambda b: (0, 0)),
            pl.BlockSpec((D_POOL, D_POOL), lambda b: (0, 0)),
            pl.BlockSpec((1, D_POOL), lambda b: (0, 0)),
        ],
        out_specs=pl.BlockSpec((1, ln, D_POOL), lambda b: (b, 0, 0)),
        out_shape=jax.ShapeDtypeStruct((bn, ln, D_POOL), BF16),
        scratch_shapes=[pltpu.VMEM((ln + 2 * POOL_PAD, D_POOL), F32)],
        compiler_params=_cparams("parallel"),
        name="pool",
    )(p, cnt, wbd, pool_scale.reshape(1, -1))


def _paired_head_order():
    order = []
    per_kv = N_Q_HEADS // N_KV_HEADS
    for j in range(N_KV_HEADS // 2):
        for i in range(per_kv):
            order += [(2 * j) * per_kv + i, (2 * j + 1) * per_kv + i]
    return order


def _even_layer(x, mod, norm1_g, norm2_g, w_in, conv_w, ln_g, ln_b, w_out, w_up, w_conv, w_down, tile):
    a, b = _inproj_even(x, mod, norm1_g.reshape(1, -1), w_in.astype(BF16), tile)
    fa = _dft(a, tile, min(512, a.shape[1] // 2))
    bb = _convmod(b, conv_w, ln_g, ln_b)
    wo = w_out.astype(BF16)
    return _block(x, fa, bb, mod, norm2_g.reshape(1, -1), wo[:D_A], wo[D_A:], w_up, w_conv, w_down, tile)


def _odd_layer_last(x, ctx, mod, modc, norm1_g, norm2_g, w_in, q_g, k_g, pool_w, pool_scale, w_out,
                    w_up, w_conv, w_down, tile):
    heads = jnp.asarray(_paired_head_order())
    qcols = (heads[:, None] * HEAD_DIM + jnp.arange(HEAD_DIM)[None, :]).reshape(-1)
    w_in_b = w_in.astype(BF16)
    w_lat = jnp.concatenate([w_in_b[:, qcols], w_in_b[:, D_Q:]], axis=1)
    g1 = norm1_g.reshape(1, -1)
    q, kt, v, u = _inproj_qk(x, mod, g1, w_lat, q_g, k_g, tile)
    pc = _inproj(ctx, modc, g1, w_in_b[:, D_Q:D_Q + 2 * D_KV], ctx.shape[1])
    kct, vc = _kvprep_ctx(pc, k_g)
    attn = _attention(q, jnp.concatenate([kct, kt], axis=2), jnp.concatenate([vc, v], axis=1), ATTN_TILE)
    pooled = _pool(u, pool_w, pool_scale)
    wo = w_out.astype(BF16)
    return _block(x, attn, pooled, mod, norm2_g.reshape(1, -1), wo[:D_Q][qcols], wo[D_Q:],
                  w_up, w_conv, w_down, tile)


def kernel(x, c, ctx, c_ctx, w_ada, b_ada, norm1_g, norm2_g, ev_w_in, ev_conv_w, ev_ln_g, ev_ln_b, ev_w_out,
           od_w_in, od_q_g, od_k_g, od_pool_w, od_pool_scale, od_w_out, ffn_w_up, ffn_conv_w, ffn_w_down):
    depth = w_ada.shape[0]
    assert depth == 2, "even layer followed by a final odd layer"
    bn = x.shape[0]
    rows = -(-(bn + 1) // 8) * 8
    cc = jnp.concatenate([c, c_ctx[None, :], jnp.zeros((rows - bn - 1, D_MODEL), F32)], axis=0)
    mods = _ada(cc, w_ada, b_ada)
    mod = [mods[i, :bn].reshape(bn, 6, D_MODEL) for i in range(depth)]
    modc = [mods[i, bn:bn + 1].reshape(1, 6, D_MODEL) for i in range(depth)]

    tile = 512
    ev = (ev_w_in[0], ev_conv_w[0], ev_ln_g[0], ev_ln_b[0], ev_w_out[0],
          ffn_w_up[0], ffn_conv_w[0], ffn_w_down[0])
    x = _even_layer(x, mod[0], norm1_g[0], norm2_g[0], *ev, tile)
    ctx = _even_layer(ctx, modc[0], norm1_g[0], norm2_g[0], *ev, ctx.shape[1])
    return _odd_layer_last(x, ctx, mod[1], modc[1], norm1_g[1], norm2_g[1], od_w_in[0], od_q_g[0], od_k_g[0],
                           od_pool_w[0], od_pool_scale[0], od_w_out[0],
                           ffn_w_up[1], ffn_conv_w[1], ffn_w_down[1], tile)
```

```python
import functools

import jax
import jax.numpy as jnp
import numpy as np
from jax import lax
from jax.experimental import pallas as pl
from jax.experimental.pallas import tpu as pltpu

F32 = jnp.float32
BF16 = jnp.bfloat16

D_MODEL = 1024
GRID_W = 64
EPS = 1e-6

A_GROUPS = 4
A_GROUP_DIM = 128
D_A = A_GROUPS * A_GROUP_DIM
D_B = 512
CONV_WIDTH = 31
CONV_PAD = 16
CONV_BLOCK = 8

HEAD_DIM = 64
N_Q_HEADS = 12
N_KV_HEADS = 4
D_Q = N_Q_HEADS * HEAD_DIM
D_KV = N_KV_HEADS * HEAD_DIM
POOL_WINDOWS = (2, 4, 8, 16)
POOL_GROUP_DIM = 64
D_POOL = len(POOL_WINDOWS) * POOL_GROUP_DIM
POOL_PAD = 16
ROPE_THETA = 10000.0
ROPE_PAIRS = HEAD_DIM // 4
Q_SCALE = HEAD_DIM ** -0.5 * 1.4426950408889634

D_FF = 2816
FFN_CHUNK = 256
INPROJ_TILE = 1024
ATTN_TILE = 1024
ATTN_SUB = 512
FFN_HALO = 16

LANES = 128
SUBLANES = 8
VMEM_LIMIT = 56 * 1024 * 1024


def _cparams(*sem):
    return pltpu.CompilerParams(dimension_semantics=sem, vmem_limit_bytes=VMEM_LIMIT)


def _sigmoid(x):
    return 1.0 / (1.0 + jnp.exp(-x))


def _silu(x):
    return x * _sigmoid(x)


def _norm_mod(x, g, shift, scale):
    ms = jnp.mean(x * x, axis=-1, keepdims=True)
    y = x * lax.rsqrt(ms + EPS) * g
    return y * (1.0 + scale) + shift


def _dot(a, b):
    return jnp.dot(a, b, preferred_element_type=F32)


def _ada_kernel(cc_ref, w_ref, b_ref, o_ref):
    s = _silu(cc_ref[...])
    o_ref[0] = _dot(s.astype(BF16), w_ref[0].astype(BF16)) + b_ref[0]


def _ada(cc, w_ada, b_ada):
    depth = w_ada.shape[0]
    rows = cc.shape[0]
    nblk = w_ada.shape[2] // D_MODEL
    return pl.pallas_call(
        _ada_kernel,
        grid=(depth, nblk),
        in_specs=[
            pl.BlockSpec((rows, D_MODEL), lambda i, n: (0, 0)),
            pl.BlockSpec((1, D_MODEL, D_MODEL), lambda i, n: (i, 0, n)),
            pl.BlockSpec((1, 1, D_MODEL), lambda i, n: (i, 0, n)),
        ],
        out_specs=pl.BlockSpec((1, rows, D_MODEL), lambda i, n: (i, 0, n)),
        out_shape=jax.ShapeDtypeStruct((depth, rows, w_ada.shape[2]), F32),
        compiler_params=_cparams("parallel", "parallel"),
        name="ada",
    )(cc, w_ada, b_ada.reshape(depth, 1, -1))


def _inproj_kernel(x_ref, mod_ref, g_ref, w_ref, o_ref):
    h = _norm_mod(x_ref[0], g_ref[...], mod_ref[0, 0:1, :], mod_ref[0, 1:2, :])
    o_ref[0] = _dot(h.astype(BF16), w_ref[...])


def _inproj(x, mod, g, w, tile):
    bn, ln, _ = x.shape
    n = w.shape[1]
    per_batch = mod.shape[0] > 1
    return pl.pallas_call(
        _inproj_kernel,
        grid=(bn, ln // tile),
        in_specs=[
            pl.BlockSpec((1, tile, D_MODEL), lambda b, t: (b, t, 0)),
            pl.BlockSpec((1, 6, D_MODEL), (lambda b, t: (b, 0, 0)) if per_batch else (lambda b, t: (0, 0, 0))),
            pl.BlockSpec((1, D_MODEL), lambda b, t: (0, 0)),
            pl.BlockSpec((D_MODEL, n), lambda b, t: (0, 0)),
        ],
        out_specs=pl.BlockSpec((1, tile, n), lambda b, t: (b, t, 0)),
        out_shape=jax.ShapeDtypeStruct((bn, ln, n), F32),
        compiler_params=_cparams("parallel", "parallel"),
        name="inproj",
    )(x, mod, g, w)


def _inproj_even_kernel(x_ref, mod_ref, g_ref, w_ref, a_ref, b_ref, *s_refs):
    h = _norm_mod(x_ref[0], g_ref[...], mod_ref[0, 0:1, :], mod_ref[0, 1:2, :])
    p = _dot(h.astype(BF16), w_ref[...])
    half = x_ref.shape[1] // 2
    for c, s_ref in enumerate(s_refs):
        cols = slice(c * LANES, (c + 1) * LANES)
        s_ref[...] = p[:, cols]
        a_ref[0, 0:half, cols] = s_ref[pl.ds(0, half, stride=2), :].astype(BF16)
        a_ref[0, half:2 * half, cols] = s_ref[pl.ds(1, half, stride=2), :].astype(BF16)
    b_ref[0] = p[:, D_A:D_A + D_B] * _sigmoid(p[:, D_A + D_B:])


def _inproj_even(x, mod, g, w, tile):
    bn, ln, _ = x.shape
    n = w.shape[1]
    per_batch = mod.shape[0] > 1
    row = lambda b, t: (b, t, 0)
    return pl.pallas_call(
        _inproj_even_kernel,
        grid=(bn, ln // tile),
        in_specs=[
            pl.BlockSpec((1, tile, D_MODEL), row),
            pl.BlockSpec((1, 6, D_MODEL), (lambda b, t: (b, 0, 0)) if per_batch else (lambda b, t: (0, 0, 0))),
            pl.BlockSpec((1, D_MODEL), lambda b, t: (0, 0)),
            pl.BlockSpec((D_MODEL, n), lambda b, t: (0, 0)),
        ],
        out_specs=[pl.BlockSpec((1, tile, D_A), row), pl.BlockSpec((1, tile, D_B), row)],
        out_shape=[jax.ShapeDtypeStruct((bn, ln, D_A), BF16), jax.ShapeDtypeStruct((bn, ln, D_B), F32)],
        scratch_shapes=[pltpu.VMEM((tile, LANES), F32) for _ in range(D_A // LANES)],
        compiler_params=_cparams("parallel", "parallel"),
        name="inproj_even",
    )(x, mod, g, w)


def _dft_kernel(a_ref, cs_ref, me_ref, mo_ref, o_ref, ze_ref, zo_ref, *, seq, tile, scale):
    half = seq // 2

    @pl.when(pl.program_id(1) == 0)
    def _():
        for g in range(A_GROUPS):
            cols = slice(g * A_GROUP_DIM, (g + 1) * A_GROUP_DIM)
            xg = _dot(a_ref[0, :, cols], cs_ref[...]).astype(BF16)
            for t in range(seq // tile):
                for z_ref, src in ((ze_ref, t * tile), (zo_ref, t * tile + tile // 2)):
                    dst = t * (tile // 2)
                    z_ref[dst:dst + tile // 2, cols] = xg[src:src + tile // 2, :A_GROUP_DIM]
                    z_ref[half + dst:half + dst + tile // 2, cols] = xg[src:src + tile // 2, A_GROUP_DIM:]

    e = _dot(me_ref[...], ze_ref[...])
    o = _dot(mo_ref[...], zo_ref[...])
    o_ref[0, 0] = ((e + o) * scale).astype(o_ref.dtype)
    o_ref[0, 1] = ((e - o) * scale).astype(o_ref.dtype)


def _dft_matrices(seq):
    def cos_sin(n, cols):
        rows = np.arange(n // 2 if cols is not None else n, dtype=np.int64)
        cols = np.arange(n, dtype=np.int64) if cols is None else cols
        ang = ((rows[:, None] * cols[None, :]) % n).astype(np.float64) * (2.0 * np.pi / n)
        return np.cos(ang), np.sin(ang)
    cc, sc = cos_sin(A_GROUP_DIM, None)
    pos = np.arange(seq, dtype=np.int64)
    mats = []
    for parity in (0, 1):
        cl, sl = cos_sin(seq, pos[parity::2])
        mats.append(jnp.asarray(np.concatenate([cl, -sl], axis=1).astype(BF16)))
    return jnp.asarray(np.concatenate([cc, sc], axis=1).astype(BF16)), mats[0], mats[1]


def _dft(a, tile, freq_block):
    bn, ln, _ = a.shape
    half = ln // 2
    cs, me, mo = _dft_matrices(ln)
    scale = float(1.0 / (ln * A_GROUP_DIM) ** 0.5)
    out = pl.pallas_call(
        functools.partial(_dft_kernel, seq=ln, tile=tile, scale=scale),
        grid=(bn, half // freq_block),
        in_specs=[
            pl.BlockSpec((1, ln, D_A), lambda b, k: (b, 0, 0)),
            pl.BlockSpec((A_GROUP_DIM, 2 * A_GROUP_DIM), lambda b, k: (0, 0)),
            pl.BlockSpec((freq_block, ln), lambda b, k: (k, 0)),
            pl.BlockSpec((freq_block, ln), lambda b, k: (k, 0)),
        ],
        out_specs=pl.BlockSpec((1, 2, freq_block, D_A), lambda b, k: (b, 0, k, 0)),
        out_shape=jax.ShapeDtypeStruct((bn, 2, half, D_A), BF16),
        scratch_shapes=[pltpu.VMEM((ln, D_A), BF16), pltpu.VMEM((ln, D_A), BF16)],
        compiler_params=_cparams("parallel", "arbitrary"),
        name="dft",
    )(a, cs, me, mo)
    return out.reshape(bn, ln, D_A)


def _convmod_kernel(*refs, seq):
    ncb = D_B // LANES
    b_refs = refs[0:ncb]
    cw_ref, lg_ref, lb_ref, o_ref = refs[ncb:ncb + 4]
    e_refs, y_refs = refs[ncb + 4:2 * ncb + 4], refs[2 * ncb + 4:3 * ncb + 4]
    grp = seq // SUBLANES
    halo = CONV_PAD * SUBLANES
    sub = lax.broadcasted_iota(jnp.int32, (halo, 1), 0) % SUBLANES

    rows = CONV_BLOCK * SUBLANES
    norm_rows = 128

    for c in range(ncb):
        cols = slice(c * LANES, (c + 1) * LANES)

        def permute(i, carry, c=c):
            for j in range(SUBLANES):
                k = i * SUBLANES + j
                dst = pl.multiple_of(halo + k * SUBLANES, SUBLANES)
                e_refs[c][pl.ds(dst, SUBLANES), :] = b_refs[c][0, pl.ds(k, SUBLANES, stride=grp), :]
            return carry

        lax.fori_loop(0, grp // SUBLANES, permute, 0)
        first = e_refs[c][halo:2 * halo, :]
        lastb = e_refs[c][seq:seq + halo, :]
        e_refs[c][0:halo, :] = jnp.where(sub == 0, 0.0, pltpu.roll(lastb, 1, 0))
        e_refs[c][seq + halo:seq + 2 * halo, :] = jnp.where(sub == SUBLANES - 1, 0.0, pltpu.roll(first, halo - 1, 0))

        taps = [cw_ref[t * SUBLANES:(t + 1) * SUBLANES, cols] for t in range(CONV_WIDTH)]

        def conv(i, carry, c=c, taps=taps):
            r0 = pl.multiple_of(i * rows, rows)
            acc = [[None, None] for _ in range(CONV_BLOCK)]
            for idx in range(CONV_BLOCK + CONV_WIDTH - 1):
                src = r0 + (idx + CONV_PAD - CONV_WIDTH // 2) * SUBLANES
                xin = e_refs[c][pl.ds(src, SUBLANES), :]
                for a in range(CONV_BLOCK):
                    t = idx - a
                    if 0 <= t < CONV_WIDTH:
                        term = xin * taps[t]
                        acc[a][t % 2] = term if acc[a][t % 2] is None else acc[a][t % 2] + term
            for a in range(CONV_BLOCK):
                y_refs[c][pl.ds(r0 + a * SUBLANES, SUBLANES), :] = acc[a][0] + acc[a][1]
            return carry

        lax.fori_loop(0, grp // CONV_BLOCK, conv, 0)

    def norm(i, carry):
        r0 = pl.multiple_of(i * norm_rows, norm_rows)
        accs = [y_refs[c][pl.ds(r0, norm_rows), :] for c in range(ncb)]
        mu = jnp.sum(sum(accs), axis=-1, keepdims=True) * (1.0 / D_B)
        cens = [a - mu for a in accs]
        var = jnp.sum(sum(a * a for a in cens), axis=-1, keepdims=True) * (1.0 / D_B)
        inv = lax.rsqrt(var + EPS)
        for c in range(ncb):
            cols = slice(c * LANES, (c + 1) * LANES)
            y_refs[c][pl.ds(r0, norm_rows), :] = _silu(cens[c] * inv * lg_ref[:, cols] + lb_ref[:, cols])
        return carry

    lax.fori_loop(0, seq // norm_rows, norm, 0, unroll=2)

    gb = grp // SUBLANES
    blk = SUBLANES * SUBLANES

    def unpermute(r2, carry):
        src = pl.multiple_of(r2 * 2 * blk, 2 * blk)
        dst = pl.multiple_of(r2 * 2 * SUBLANES, 2 * SUBLANES)
        for q in range(SUBLANES):
            for c in range(ncb):
                two = [y_refs[c][pl.ds(src + h * blk + q, SUBLANES, stride=SUBLANES), :] for h in range(2)]
                o_ref[0, pl.ds(dst + q * grp, 2 * SUBLANES), c * LANES:(c + 1) * LANES] = (
                    jnp.concatenate(two, axis=0).astype(o_ref.dtype))
        return carry

    lax.fori_loop(0, gb // 2, unpermute, 0)


def _convmod(b, conv_w, ln_g, ln_b):
    bn, ln, _ = b.shape
    ncb = D_B // LANES
    cw = jnp.repeat(conv_w, SUBLANES, axis=0)
    col_spec = lambda cb: pl.BlockSpec((1, ln, LANES), lambda i: (i, 0, cb))
    return pl.pallas_call(
        functools.partial(_convmod_kernel, seq=ln),
        grid=(bn,),
        in_specs=[col_spec(c) for c in range(ncb)] + [
            pl.BlockSpec((CONV_WIDTH * SUBLANES, D_B), lambda b: (0, 0)),
            pl.BlockSpec((1, D_B), lambda b: (0, 0)),
            pl.BlockSpec((1, D_B), lambda b: (0, 0)),
        ],
        out_specs=pl.BlockSpec((1, ln, D_B), lambda b: (b, 0, 0)),
        out_shape=jax.ShapeDtypeStruct((bn, ln, D_B), BF16),
        scratch_shapes=[pltpu.VMEM((ln + 2 * CONV_PAD * SUBLANES, LANES), F32) for _ in range(ncb)]
                       + [pltpu.VMEM((ln, LANES), F32) for _ in range(ncb)],
        compiler_params=_cparams("parallel"),
        name="convmod",
    )(*([b] * ncb), cw, ln_g.reshape(1, -1), ln_b.reshape(1, -1))


def _block_kernel(xp_ref, x_ref, xn_ref, ap_ref, a_ref, an_ref, bp_ref, b_ref, bn_ref, mod_ref, g_ref,
                  w1_ref, w2_ref, wu_ref, cw_ref, wd_ref, o_ref, h_ref, x1_ref, act_ref, *, tile):
    t = pl.program_id(1)
    last = pl.num_programs(1) - 1
    g = g_ref[...]
    gate1 = mod_ref[0, 2:3, :]
    shift, scale, gate2 = mod_ref[0, 3:4, :], mod_ref[0, 4:5, :], mod_ref[0, 5:6, :]
    ext = tile + 2 * FFN_HALO

    def mixed(xr, ar, br):
        return xr[0] + gate1 * (_dot(ar[0], w1_ref[...]) + _dot(br[0], w2_ref[...]))

    x1_ref[...] = mixed(x_ref, a_ref, b_ref)
    hp = _norm_mod(mixed(xp_ref, ap_ref, bp_ref), g, shift, scale)
    hn = _norm_mod(mixed(xn_ref, an_ref, bn_ref), g, shift, scale)
    h_ref[0:FFN_HALO, :] = jnp.where(t > 0, hp, 0.0).astype(BF16)
    h_ref[FFN_HALO:FFN_HALO + tile, :] = _norm_mod(x1_ref[...], g, shift, scale).astype(BF16)
    h_ref[FFN_HALO + tile:ext, :] = jnp.where(t < last, hn, 0.0).astype(BF16)

    def conv3(up, cw):
        rows = slice(FFN_HALO, FFN_HALO + tile)
        prev = pltpu.roll(up, 1, 0)[rows, :]
        nxt = pltpu.roll(up, ext - 1, 0)[rows, :]
        return prev * cw[0:1, :] + up[rows, :] * cw[1:2, :] + nxt * cw[2:3, :]

    hx = h_ref[...]
    for j in range(D_FF // FFN_CHUNK):
        gcols = slice(j * FFN_CHUNK, (j + 1) * FFN_CHUNK)
        vcols = slice(D_FF + j * FFN_CHUNK, D_FF + (j + 1) * FFN_CHUNK)
        ug = conv3(_dot(hx, wu_ref[:, gcols]), cw_ref[:, gcols])
        uv = conv3(_dot(hx, wu_ref[:, vcols]), cw_ref[:, vcols])
        act_ref[:, gcols] = (_silu(ug) * uv).astype(BF16)
    o_ref[0] = x1_ref[...] + gate2 * _dot(act_ref[...], wd_ref[...])


def _block(x, y1, y2, mod, g, w1, w2, w_up, w_conv, w_down, tile):
    bn, ln, _ = x.shape
    d1, d2 = y1.shape[2], y2.shape[2]
    per_batch = mod.shape[0] > 1
    hb = tile // FFN_HALO
    nhb = ln // FFN_HALO
    wc = jnp.pad(w_conv, ((0, SUBLANES - w_conv.shape[0]), (0, 0)))
    const2 = lambda b, t: (0, 0)
    const3 = lambda b, t: (0, 0, 0)
    prev = lambda b, t: (b, jnp.maximum(t * hb - 1, 0), 0)
    main = lambda b, t: (b, t, 0)
    nxt = lambda b, t: (b, jnp.minimum((t + 1) * hb, nhb - 1), 0)

    def rows3(d):
        return [pl.BlockSpec((1, FFN_HALO, d), prev), pl.BlockSpec((1, tile, d), main),
                pl.BlockSpec((1, FFN_HALO, d), nxt)]

    return pl.pallas_call(
        functools.partial(_block_kernel, tile=tile),
        grid=(bn, ln // tile),
        in_specs=rows3(D_MODEL) + rows3(d1) + rows3(d2) + [
            pl.BlockSpec((1, 6, D_MODEL), (lambda b, t: (b, 0, 0)) if per_batch else const3),
            pl.BlockSpec((1, D_MODEL), const2),
            pl.BlockSpec((d1, D_MODEL), const2),
            pl.BlockSpec((d2, D_MODEL), const2),
            pl.BlockSpec((D_MODEL, 2 * D_FF), const2),
            pl.BlockSpec((SUBLANES, 2 * D_FF), const2),
            pl.BlockSpec((D_FF, D_MODEL), const2),
        ],
        out_specs=pl.BlockSpec((1, tile, D_MODEL), main),
        out_shape=jax.ShapeDtypeStruct((bn, ln, D_MODEL), F32),
        scratch_shapes=[pltpu.VMEM((tile + 2 * FFN_HALO, D_MODEL), BF16),
                        pltpu.VMEM((tile, D_MODEL), F32),
                        pltpu.VMEM((tile, D_FF), BF16)],
        compiler_params=_cparams("parallel", "parallel"),
        name="block",
    )(x, x, x, y1, y1, y1, y2, y2, y2, mod, g, w1, w2, w_up.astype(BF16), wc, w_down.astype(BF16))


def _head_norm(xb, ones_ref, gain):
    sq = xb * xb
    hi = sq.astype(BF16)
    lo = (sq - hi.astype(F32)).astype(BF16)
    ss = _dot(jnp.concatenate([hi, lo], axis=1), ones_ref[...])
    return xb * lax.rsqrt(ss * (1.0 / HEAD_DIM) + EPS) * gain


def _rope(y, cos, sin, first):
    partner = jnp.where(first, pltpu.roll(y, LANES - ROPE_PAIRS, 1), pltpu.roll(y, ROPE_PAIRS, 1))
    return y * cos + partner * sin


def _inproj_qk_kernel(x_ref, mod_ref, g_ref, w_ref, cos_ref, sin_ref, gq_ref, gk_ref, ones_ref,
                      q_ref, kt_ref, v_ref, u_ref):
    h = _norm_mod(x_ref[0], g_ref[...], mod_ref[0, 0:1, :], mod_ref[0, 1:2, :])
    p = _dot(h.astype(BF16), w_ref[...])
    lane = lax.broadcasted_iota(jnp.int32, (1, LANES), 1)
    first = (lane % (2 * ROPE_PAIRS)) < ROPE_PAIRS
    cos, sin = cos_ref[...], sin_ref[...]
    for c in range(D_Q // LANES):
        cols = slice(c * LANES, (c + 1) * LANES)
        y = _head_norm(p[:, cols], ones_ref, gq_ref[...])
        q_ref[0, :, cols] = (_rope(y, cos, sin, first) * Q_SCALE).astype(BF16)
    for c in range(D_KV // LANES):
        src = slice(D_Q + c * LANES, D_Q + (c + 1) * LANES)
        y = _head_norm(p[:, src], ones_ref, gk_ref[...])
        kt_ref[0, c * LANES:(c + 1) * LANES, :] = _rope(y, cos, sin, first).T.astype(BF16)
    v_ref[0] = p[:, D_Q + D_KV:D_Q + 2 * D_KV].astype(BF16)
    u_ref[0] = p[:, D_Q + 2 * D_KV:]


def _kvprep_ctx_kernel(p_ref, gk_ref, ones_ref, kt_ref, v_ref):
    for c in range(D_KV // LANES):
        cols = slice(c * LANES, (c + 1) * LANES)
        kt_ref[0, cols, :] = _head_norm(p_ref[0, :, cols], ones_ref, gk_ref[...]).T.astype(BF16)
    v_ref[0] = p_ref[0, :, D_KV:2 * D_KV].astype(BF16)


def _group_ones():
    r = np.arange(LANES) // HEAD_DIM
    ones = r[:, None] == r[None, :]
    return jnp.asarray(np.concatenate([ones, ones], axis=0), BF16)


def _rope_tables(seq):
    t = np.arange(seq)
    freqs = ROPE_THETA ** (-np.arange(ROPE_PAIRS, dtype=np.float64) / ROPE_PAIRS)
    ang_r = (t // GRID_W).astype(np.float64)[:, None] * freqs
    ang_c = (t % GRID_W).astype(np.float64)[:, None] * freqs
    cos = np.concatenate([np.cos(ang_r)] * 2 + [np.cos(ang_c)] * 2, axis=1)
    sin = np.concatenate([-np.sin(ang_r), np.sin(ang_r), -np.sin(ang_c), np.sin(ang_c)], axis=1)
    reps = (1, LANES // HEAD_DIM)
    return jnp.asarray(np.tile(cos, reps), F32), jnp.asarray(np.tile(sin, reps), F32)


def _inproj_qk(x, mod, g, w, q_g, k_g, tile):
    bn, ln, _ = x.shape
    n = w.shape[1]
    cos, sin = _rope_tables(ln)
    gq = jnp.tile(q_g, LANES // HEAD_DIM).reshape(1, LANES)
    gk = jnp.tile(k_g, LANES // HEAD_DIM).reshape(1, LANES)
    row = lambda b, t: (b, t, 0)
    tab = lambda b, t: (t, 0)
    const = lambda b, t: (0, 0)
    return pl.pallas_call(
        _inproj_qk_kernel,
        grid=(bn, ln // tile),
        in_specs=[
            pl.BlockSpec((1, tile, D_MODEL), row),
            pl.BlockSpec((1, 6, D_MODEL), lambda b, t: (b, 0, 0)),
            pl.BlockSpec((1, D_MODEL), const),
            pl.BlockSpec((D_MODEL, n), const),
            pl.BlockSpec((tile, LANES), tab),
            pl.BlockSpec((tile, LANES), tab),
            pl.BlockSpec((1, LANES), const),
            pl.BlockSpec((1, LANES), const),
            pl.BlockSpec((2 * LANES, LANES), const),
        ],
        out_specs=[pl.BlockSpec((1, tile, D_Q), row),
                   pl.BlockSpec((1, D_KV, tile), lambda b, t: (b, 0, t)),
                   pl.BlockSpec((1, tile, D_KV), row),
                   pl.BlockSpec((1, tile, D_POOL), row)],
        out_shape=[jax.ShapeDtypeStruct((bn, ln, D_Q), BF16),
                   jax.ShapeDtypeStruct((bn, D_KV, ln), BF16),
                   jax.ShapeDtypeStruct((bn, ln, D_KV), BF16),
                   jax.ShapeDtypeStruct((bn, ln, D_POOL), F32)],
        compiler_params=_cparams("parallel", "parallel"),
        name="inproj_qk",
    )(x, mod, g, w, cos, sin, gq, gk, _group_ones())


def _kvprep_ctx(pc, k_g):
    bn, lc, n = pc.shape
    gk = jnp.tile(k_g, LANES // HEAD_DIM).reshape(1, LANES)
    return pl.pallas_call(
        _kvprep_ctx_kernel,
        grid=(bn,),
        in_specs=[
            pl.BlockSpec((1, lc, n), lambda b: (b, 0, 0)),
            pl.BlockSpec((1, LANES), lambda b: (0, 0)),
            pl.BlockSpec((2 * LANES, LANES), lambda b: (0, 0)),
        ],
        out_specs=[pl.BlockSpec((1, D_KV, lc), lambda b: (b, 0, 0)),
                   pl.BlockSpec((1, lc, D_KV), lambda b: (b, 0, 0))],
        out_shape=[jax.ShapeDtypeStruct((bn, D_KV, lc), BF16),
                   jax.ShapeDtypeStruct((bn, lc, D_KV), BF16)],
        compiler_params=_cparams("parallel"),
        name="kvprep_ctx",
    )(pc, gk, _group_ones())


def _attn_kernel(q_ref, kt_ref, v_ref, o_ref, vlo_ref, vhi_ref):
    low = lax.broadcasted_iota(jnp.int32, (1, LANES), 1) < HEAD_DIM

    @pl.when(pl.program_id(2) == 0)
    def _():
        v = v_ref[0]
        one = jnp.ones_like(v)
        vlo_ref[...] = jnp.where(low, v, one)
        vhi_ref[...] = jnp.where(low, one, v)

    kt = kt_ref[0]
    for r0 in range(0, q_ref.shape[1], ATTN_SUB):
        rows = slice(r0, r0 + ATTN_SUB)
        for c in range(q_ref.shape[2] // LANES):
            cols = slice(c * LANES, (c + 1) * LANES)
            qp = q_ref[0, rows, cols]
            zero = jnp.zeros_like(qp)
            halves = []
            for qm, vm_ref in ((jnp.where(low, qp, zero), vlo_ref), (jnp.where(low, zero, qp), vhi_ref)):
                s = _dot(qm, kt)
                e = jnp.exp2(s - jnp.max(s, axis=-1, keepdims=True))
                o = _dot(e.astype(BF16), vm_ref[...])
                halves.append(o / pltpu.roll(o, HEAD_DIM, 1))
            o_ref[0, rows, cols] = jnp.where(low, halves[0], halves[1]).astype(o_ref.dtype)


def _attention(q, kt_all, v_all, tile):
    bn, ln, _ = q.shape
    lk = v_all.shape[1]
    qw = D_Q // 2
    kw = D_KV // 2
    return pl.pallas_call(
        _attn_kernel,
        grid=(bn, 2, ln // tile),
        in_specs=[
            pl.BlockSpec((1, tile, qw), lambda b, j, t: (b, t, j)),
            pl.BlockSpec((1, kw, lk), lambda b, j, t: (b, j, 0)),
            pl.BlockSpec((1, lk, kw), lambda b, j, t: (b, 0, j)),
        ],
        out_specs=pl.BlockSpec((1, tile, qw), lambda b, j, t: (b, t, j)),
        out_shape=jax.ShapeDtypeStruct((bn, ln, D_Q), BF16),
        scratch_shapes=[pltpu.VMEM((lk, kw), BF16), pltpu.VMEM((lk, kw), BF16)],
        compiler_params=_cparams("parallel", "parallel", "arbitrary"),
        name="attention",
    )(q, kt_all, v_all)


def _pool_kernel(u_ref, cnt_ref, w_ref, sc_ref, o_ref, pad_ref, *, seq):
    n = seq + 2 * POOL_PAD
    zeros = jnp.zeros((POOL_PAD, D_POOL), F32)
    pad_ref[0:POOL_PAD, :] = zeros
    pad_ref[seq + POOL_PAD:n, :] = zeros
    pad_ref[POOL_PAD:seq + POOL_PAD, :] = u_ref[0]
    a = pad_ref[...]
    w2 = a + pltpu.roll(a, 1, 0)
    w4 = pltpu.roll(w2, 1, 0) + pltpu.roll(w2, n - 1, 0)
    w8 = pltpu.roll(w4, 2, 0) + pltpu.roll(w4, n - 2, 0)
    w16 = pltpu.roll(w8, 4, 0) + pltpu.roll(w8, n - 4, 0)
    lane = lax.broadcasted_iota(jnp.int32, (1, D_POOL), 1)
    g = lane // POOL_GROUP_DIM
    win = jnp.where(g == 0, w2, jnp.where(g == 1, w4, jnp.where(g == 2, w8, w16)))
    u = u_ref[0]
    pooled = win[POOL_PAD:seq + POOL_PAD, :] / cnt_ref[...] - u
    o_ref[0] = (_dot(pooled.astype(BF16), w_ref[...]) * sc_ref[...]).astype(o_ref.dtype)


def _pool(p, pool_w, pool_scale):
    bn, ln, n = p.shape
    t = np.arange(ln)
    cnt = jnp.asarray(np.concatenate(
        [np.broadcast_to((np.minimum(t + w // 2, ln) - np.maximum(t - w // 2, 0))[:, None], (ln, POOL_GROUP_DIM))
         for w in POOL_WINDOWS], axis=1), F32)
    wbd = jax.scipy.linalg.block_diag(*[pool_w[i] for i in range(pool_w.shape[0])]).astype(BF16)
    return pl.pallas_call(
        functools.partial(_pool_kernel, seq=ln),
        grid=(bn,),
        in_specs=[
            pl.BlockSpec((1, ln, D_POOL), lambda b: (b, 0, n // D_POOL - 1)),
            pl.BlockSpec((ln, D_POOL), lambda b: (0, 0)),
            pl.BlockSpec((D_POOL, D_POOL), lambda b: (0, 0)),
            pl.BlockSpec((1, D_POOL), lambda b: (0, 0)),
        ],
        out_specs=pl.BlockSpec((1, ln, D_POOL), lambda b: (b, 0, 0)),
        out_shape=jax.ShapeDtypeStruct((bn, ln, D_POOL), BF16),
        scratch_shapes=[pltpu.VMEM((ln + 2 * POOL_PAD, D_POOL), F32)],
        compiler_params=_cparams("parallel"),
        name="pool",
    )(p, cnt, wbd, pool_scale.reshape(1, -1))


def _paired_head_order():
    order = []
    per_kv = N_Q_HEADS // N_KV_HEADS
    for j in range(N_KV_HEADS // 2):
        for i in range(per_kv):
            order += [(2 * j) * per_kv + i, (2 * j + 1) * per_kv + i]
    return order


def _even_layer(x, mod, norm1_g, norm2_g, w_in, conv_w, ln_g, ln_b, w_out, w_up, w_conv, w_down, tile):
    in_tile = min(INPROJ_TILE, x.shape[1])
    a, b = _inproj_even(x, mod, norm1_g.reshape(1, -1), w_in.astype(BF16), in_tile)
    fa = _dft(a, in_tile, min(512, a.shape[1] // 2))
    bb = _convmod(b, conv_w, ln_g, ln_b)
    wo = w_out.astype(BF16)
    return _block(x, fa, bb, mod, norm2_g.reshape(1, -1), wo[:D_A], wo[D_A:], w_up, w_conv, w_down, tile)


def _odd_layer_last(x, ctx, mod, modc, norm1_g, norm2_g, w_in, q_g, k_g, pool_w, pool_scale, w_out,
                    w_up, w_conv, w_down, tile):
    heads = jnp.asarray(_paired_head_order())
    qcols = (heads[:, None] * HEAD_DIM + jnp.arange(HEAD_DIM)[None, :]).reshape(-1)
    w_in_b = w_in.astype(BF16)
    w_lat = jnp.concatenate([w_in_b[:, qcols], w_in_b[:, D_Q:]], axis=1)
    g1 = norm1_g.reshape(1, -1)
    q, kt, v, u = _inproj_qk(x, mod, g1, w_lat, q_g, k_g, min(INPROJ_TILE, x.shape[1]))
    pc = _inproj(ctx, modc, g1, w_in_b[:, D_Q:D_Q + 2 * D_KV], ctx.shape[1])
    kct, vc = _kvprep_ctx(pc, k_g)
    attn = _attention(q, jnp.concatenate([kct, kt], axis=2), jnp.concatenate([vc, v], axis=1), ATTN_TILE)
    pooled = _pool(u, pool_w, pool_scale)
    wo = w_out.astype(BF16)
    return _block(x, attn, pooled, mod, norm2_g.reshape(1, -1), wo[:D_Q][qcols], wo[D_Q:],
                  w_up, w_conv, w_down, tile)


def kernel(x, c, ctx, c_ctx, w_ada, b_ada, norm1_g, norm2_g, ev_w_in, ev_conv_w, ev_ln_g, ev_ln_b, ev_w_out,
           od_w_in, od_q_g, od_k_g, od_pool_w, od_pool_scale, od_w_out, ffn_w_up, ffn_conv_w, ffn_w_down):
    depth = w_ada.shape[0]
    assert depth == 2, "even layer followed by a final odd layer"
    bn = x.shape[0]
    rows = -(-(bn + 1) // 8) * 8
    cc = jnp.concatenate([c, c_ctx[None, :], jnp.zeros((rows - bn - 1, D_MODEL), F32)], axis=0)
    mods = _ada(cc, w_ada, b_ada)
    mod = [mods[i, :bn].reshape(bn, 6, D_MODEL) for i in range(depth)]
    modc = [mods[i, bn:bn + 1].reshape(1, 6, D_MODEL) for i in range(depth)]

    tile = 512
    ev = (ev_w_in[0], ev_conv_w[0], ev_ln_g[0], ev_ln_b[0], ev_w_out[0],
          ffn_w_up[0], ffn_conv_w[0], ffn_w_down[0])
    x = _even_layer(x, mod[0], norm1_g[0], norm2_g[0], *ev, tile)
    ctx = _even_layer(ctx, modc[0], norm1_g[0], norm2_g[0], *ev, ctx.shape[1])
    return _odd_layer_last(x, ctx, mod[1], modc[1], norm1_g[1], norm2_g[1], od_w_in[0], od_q_g[0], od_k_g[0],
                           od_pool_w[0], od_pool_scale[0], od_w_out[0],
                           ffn_w_up[1], ffn_conv_w[1], ffn_w_down[1], tile)
```
